```python
import math
import jax, jax.numpy as jnp
from jax import lax
import numpy as np

D_MODEL = 4096
BATCH = 4
SEQ = 2048
DEPTH = 1
DEC_BATCH = 128
DEC_SEQ = 4
PAST_LEN = 16384
PAGE_SIZE = 128

CONV_CH = D_MODEL // 2
CONV_K = 31
SSM_HEAD_DIM = 64
SSM_HEADS = D_MODEL // SSM_HEAD_DIM
D_SSM = SSM_HEADS * SSM_HEAD_DIM
SSM_GROUPS = 8
D_STATE = 128
SSM_CONV_K = 4
SSD_CHUNK = 128
XBC_DIM = D_SSM + 2 * SSM_GROUPS * D_STATE
MIX_WIDTH = CONV_CH + D_SSM
IN_COLS = 2 * CONV_CH + D_SSM + XBC_DIM + SSM_HEADS
N_EXPERTS = 32
TOP_K = 4
D_FF = D_MODEL
SWIGLU_LIMIT = 7.0
SWIGLU_ALPHA = 1.702
MOE_BLOCK = 128
N_MOD = 6
RMS_EPS = 1e-5
LN_EPS = 1e-5

kernel_name = 'hymba_conformer_ssd_moe_adaln_step'


def rms_norm(x, w):
    xf = x.astype(jnp.float32)
    y = xf * lax.rsqrt(jnp.mean(xf * xf, axis=-1, keepdims=True) + RMS_EPS)
    return (y * w.astype(jnp.float32)).astype(x.dtype)


def layer_norm(x, w, b):
    xf = x.astype(jnp.float32)
    mu = jnp.mean(xf, axis=-1, keepdims=True)
    xc = xf - mu
    var = jnp.mean(xc * xc, axis=-1, keepdims=True)
    y = xc * lax.rsqrt(var + LN_EPS) * w.astype(jnp.float32) + b.astype(jnp.float32)
    return y.astype(x.dtype)


def causal_depthwise_conv(x_full, w, b):
    ch = x_full.shape[-1]
    y = lax.conv_general_dilated(x_full, w[:, None, :].astype(x_full.dtype), window_strides=(1,),
                                 padding='VALID', dimension_numbers=('NWC', 'WIO', 'NWC'),
                                 feature_group_count=ch)
    return y + b.astype(x_full.dtype)


def ssd_scan(x, dt, a, bm, cm, h0):
    f32 = jnp.float32
    bsz, seq = x.shape[:2]
    q = SSD_CHUNK if seq % SSD_CHUNK == 0 else seq
    nc = seq // q
    r = SSM_HEADS // SSM_GROUPS
    xc = x.astype(f32).reshape(bsz, nc, q, SSM_GROUPS, r, SSM_HEAD_DIM)
    dtc = dt.astype(f32).reshape(bsz, nc, q, SSM_GROUPS, r)
    bc = bm.astype(f32).reshape(bsz, nc, q, SSM_GROUPS, D_STATE)
    cc = cm.astype(f32).reshape(bsz, nc, q, SSM_GROUPS, D_STATE)
    a_gr = a.astype(f32).reshape(SSM_GROUPS, r)
    cum = jnp.cumsum(dtc * a_gr, axis=2)
    cum_t = jnp.moveaxis(cum, 2, -1)
    diff = cum_t[..., :, None] - cum_t[..., None, :]
    causal = jnp.tril(jnp.ones((q, q), dtype=bool))
    decay = jnp.exp(jnp.where(causal, diff, -jnp.inf))
    cb = jnp.einsum('bcqgn,bckgn->bcgqk', cc, bc)
    dt_t = jnp.moveaxis(dtc, 2, -1)
    w_intra = cb[:, :, :, None] * decay * dt_t[..., None, :]
    y_diag = jnp.einsum('bcgrqk,bckgrp->bcqgrp', w_intra, xc)
    decay_end = jnp.exp(cum[:, :, -1:] - cum)
    states = jnp.einsum('bcqgn,bcqgr,bcqgrp->bcgrpn', bc, decay_end * dtc, xc)
    chunk_decay = jnp.exp(cum[:, :, -1])
    h0g = h0.astype(f32).reshape(bsz, SSM_GROUPS, r, SSM_HEAD_DIM, D_STATE)

    def step(h, inp):
        dec, st = inp
        return dec[..., None, None] * h + st, h

    h_final, h_in = lax.scan(step, h0g, (jnp.moveaxis(chunk_decay, 1, 0), jnp.moveaxis(states, 1, 0)))
    h_in = jnp.moveaxis(h_in, 0, 1)
    y_off = jnp.einsum('bcqgn,bcgrpn->bcqgrp', cc, h_in) * jnp.exp(cum)[..., None]
    y = (y_diag + y_off).reshape(bsz, seq, SSM_HEADS, SSM_HEAD_DIM)
    return y, h_final.reshape(bsz, SSM_HEADS, SSM_HEAD_DIM, D_STATE)


def hybrid_mixer(h, buf_a, buf_b, ssm_h0, layer, p):
    f32 = jnp.float32
    bsz, seq, _ = h.shape
    proj = h @ p['w_in'][layer]
    s0 = 2 * CONV_CH
    s1 = s0 + D_SSM
    s2 = s1 + XBC_DIM
    glu_in = proj[..., :s0] + p['b_glu'][layer]
    z = proj[..., s0:s1]
    xbc = proj[..., s1:s2]
    dt_raw = proj[..., s2:]
    u = glu_in[..., :CONV_CH] * jax.nn.sigmoid(glu_in[..., CONV_CH:])
    u_full = jnp.concatenate([buf_a.astype(u.dtype), u], axis=1)
    v = causal_depthwise_conv(u_full, p['w_dw'][layer], p['b_dw'][layer])
    out_a = jax.nn.silu(layer_norm(v, p['ln_w'][layer], p['ln_b'][layer]))
    new_buf_a = u_full[:, seq:]
    xbc_full = jnp.concatenate([buf_b.astype(xbc.dtype), xbc], axis=1)
    xbc_c = jax.nn.silu(causal_depthwise_conv(xbc_full, p['w_xbc_conv'][layer], p['b_xbc_conv'][layer]))
    new_buf_b = xbc_full[:, seq:]
    xs = xbc_c[..., :D_SSM].reshape(bsz, seq, SSM_HEADS, SSM_HEAD_DIM)
    bm = xbc_c[..., D_SSM:D_SSM + SSM_GROUPS * D_STATE].reshape(bsz, seq, SSM_GROUPS, D_STATE)
    cm = xbc_c[..., D_SSM + SSM_GROUPS * D_STATE:].reshape(bsz, seq, SSM_GROUPS, D_STATE)
    dt = jax.nn.softplus(dt_raw.astype(f32) + p['dt_bias'][layer].astype(f32))
    a = -jnp.exp(p['a_log'][layer].astype(f32))
    y, h_fin = ssd_scan(xs, dt, a, bm, cm, ssm_h0)
    y = y + xs.astype(f32) * p['d_skip'][layer].astype(f32)[:, None]
    yg = (y.reshape(bsz, seq, D_SSM) * jax.nn.silu(z.astype(f32))).reshape(bsz, seq, SSM_GROUPS, D_SSM // SSM_GROUPS)
    yg = yg * lax.rsqrt(jnp.mean(yg * yg, axis=-1, keepdims=True) + RMS_EPS)
    out_b = (yg.reshape(bsz, seq, D_SSM) * p['ssm_norm_w'][layer].astype(f32)).astype(h.dtype)
    mixed = jnp.concatenate([out_a, out_b], axis=-1) @ p['w_out'][layer]
    return mixed, new_buf_a, new_buf_b, h_fin.astype(ssm_h0.dtype)


def moe_ffn(h, layer, p):
    f32 = jnp.float32
    n_tok, d = h.shape
    logits = (h @ p['w_router'][layer] + p['b_router'][layer]).astype(f32)
    top_logit, top_e = lax.top_k(logits, TOP_K)
    probs = jax.nn.softmax(top_logit, axis=-1)
    n_assign = n_tok * TOP_K
    flat_e = top_e.reshape(-1).astype(jnp.int32)
    flat_tok = jnp.arange(n_assign, dtype=jnp.int32) // TOP_K
    order = jnp.argsort(flat_e)
    sorted_e = flat_e[order]
    counts = jnp.bincount(flat_e, length=N_EXPERTS).astype(jnp.int32)
    padded = (counts + MOE_BLOCK - 1) // MOE_BLOCK * MOE_BLOCK
    pad_end = jnp.cumsum(padded)
    pad_start = pad_end - padded
    start = jnp.cumsum(counts) - counts
    dest = pad_start[sorted_e] + jnp.arange(n_assign, dtype=jnp.int32) - start[sorted_e]
    n_blocks = (n_assign + N_EXPERTS * (MOE_BLOCK - 1) + MOE_BLOCK - 1) // MOE_BLOCK
    n_rows = n_blocks * MOE_BLOCK
    row_tok = jnp.full((n_rows,), n_tok, jnp.int32).at[dest].set(flat_tok[order])
    row_w = jnp.zeros((n_rows,), f32).at[dest].set(probs.reshape(-1)[order])
    block_start = jnp.arange(n_blocks, dtype=jnp.int32) * MOE_BLOCK
    block_e = jnp.minimum(jnp.searchsorted(pad_end, block_start, side='right'), N_EXPERTS - 1).astype(jnp.int32)
    h_pad = jnp.concatenate([h, jnp.zeros((1, d), h.dtype)], axis=0)
    xb = h_pad[row_tok].reshape(n_blocks, MOE_BLOCK, d)
    w_gate, b_gate = p['w_gate'], p['b_gate']
    w_up, b_up = p['w_up'], p['b_up']
    w_down, b_down = p['w_down'], p['b_down']

    def expert_block(args):
        xe, e = args
        g = xe @ w_gate[layer, e] + b_gate[layer, e]
        u = xe @ w_up[layer, e] + b_up[layer, e]
        g = jnp.minimum(g, SWIGLU_LIMIT)
        u = jnp.clip(u, -SWIGLU_LIMIT, SWIGLU_LIMIT)
        act = (u + 1) * (g * jax.nn.sigmoid(SWIGLU_ALPHA * g))
        return act @ w_down[layer, e] + b_down[layer, e]

    out = lax.map(expert_block, (xb, block_e)).reshape(n_rows, d)
    out = out * row_w[:, None].astype(out.dtype)
    return jax.ops.segment_sum(out, row_tok, num_segments=n_tok + 1)[:n_tok]


def run_group(x, c, buf_a, buf_b, ssm_state, p):
    bsz, seq, _ = x.shape
    new_a, new_b, new_s = [], [], []
    for layer in range(DEPTH):
        mod = jax.nn.silu(c) @ p['w_ada'][layer] + p['b_ada'][layer]
        shift1, scale1, gate1, shift2, scale2, gate2 = jnp.split(mod[:, None, :], N_MOD, axis=-1)
        hn = rms_norm(x, p['norm1_w'][layer]) * (1 + scale1) + shift1
        mixed, ba, bb, st = hybrid_mixer(hn, buf_a[layer], buf_b[layer], ssm_state[layer], layer, p)
        x = x + gate1 * mixed
        hn = rms_norm(x, p['norm2_w'][layer]) * (1 + scale2) + shift2
        ffn = moe_ffn(hn.reshape(bsz * seq, D_MODEL), layer, p).reshape(bsz, seq, D_MODEL)
        x = x + gate2 * ffn
        new_a.append(ba)
        new_b.append(bb)
        new_s.append(st)
    y = rms_norm(x, p['final_norm_w'])
    return y, jnp.stack(new_a), jnp.stack(new_b), jnp.stack(new_s)


def setup_inputs(seed: int = 0) -> dict:
    key = jax.random.key(seed)
    ks = jax.random.split(key, 40)
    f32 = jnp.float32

    def nrm(k, shape, scale):
        return jax.random.normal(k, shape, f32) * scale

    dt0 = jnp.exp(jax.random.uniform(ks[20], (DEPTH, SSM_HEADS), f32, math.log(1e-3), math.log(1e-1)))
    return {
        'x_prompt': nrm(ks[0], (BATCH, SEQ, D_MODEL), 1.0),
        'x_sample': nrm(ks[1], (DEC_BATCH, DEC_SEQ, D_MODEL), 1.0),
        'state_conv_a': nrm(ks[2], (DEPTH, DEC_BATCH, CONV_K - 1, CONV_CH), 0.5),
        'state_conv_ssm': nrm(ks[3], (DEPTH, DEC_BATCH, SSM_CONV_K - 1, XBC_DIM), 1.0),
        'state_ssm': nrm(ks[4], (DEPTH, DEC_BATCH, SSM_HEADS, SSM_HEAD_DIM, D_STATE), 0.5),
        'c_prompt': nrm(ks[5], (BATCH, D_MODEL), 1.0),
        'c_sample': nrm(ks[6], (DEC_BATCH, D_MODEL), 1.0),
        'norm1_w': 1.0 + nrm(ks[7], (DEPTH, D_MODEL), 0.02),
        'norm2_w': 1.0 + nrm(ks[8], (DEPTH, D_MODEL), 0.02),
        'final_norm_w': 1.0 + nrm(ks[9], (D_MODEL,), 0.02),
        'w_ada': nrm(ks[10], (DEPTH, D_MODEL, N_MOD * D_MODEL), 0.5 * D_MODEL ** -0.5),
        'b_ada': nrm(ks[11], (DEPTH, N_MOD * D_MODEL), 0.02),
        'w_in': nrm(ks[12], (DEPTH, D_MODEL, IN_COLS), D_MODEL ** -0.5),
        'b_glu': nrm(ks[13], (DEPTH, 2 * CONV_CH), 0.02),
        'w_dw': nrm(ks[14], (DEPTH, CONV_K, CONV_CH), CONV_K ** -0.5),
        'b_dw': nrm(ks[15], (DEPTH, CONV_CH), 0.02),
        'ln_w': 1.0 + nrm(ks[16], (DEPTH, CONV_CH), 0.02),
        'ln_b': nrm(ks[17], (DEPTH, CONV_CH), 0.02),
        'w_xbc_conv': nrm(ks[18], (DEPTH, SSM_CONV_K, XBC_DIM), SSM_CONV_K ** -0.5),
        'b_xbc_conv': nrm(ks[19], (DEPTH, XBC_DIM), 0.02),
        'dt_bias': dt0 + jnp.log(-jnp.expm1(-dt0)),
        'a_log': jnp.log(jax.random.uniform(ks[21], (DEPTH, SSM_HEADS), f32, 1.0, 16.0)),
        'd_skip': 1.0 + nrm(ks[22], (DEPTH, SSM_HEADS), 0.1),
        'ssm_norm_w': 1.0 + nrm(ks[23], (DEPTH, D_SSM), 0.02),
        'w_out': nrm(ks[24], (DEPTH, MIX_WIDTH, D_MODEL), MIX_WIDTH ** -0.5),
        'w_router': nrm(ks[25], (DEPTH, D_MODEL, N_EXPERTS), D_MODEL ** -0.5),
        'b_router': nrm(ks[26], (DEPTH, N_EXPERTS), 0.01),
        'w_gate': nrm(ks[27], (DEPTH, N_EXPERTS, D_MODEL, D_FF), D_MODEL ** -0.5),
        'b_gate': nrm(ks[28], (DEPTH, N_EXPERTS, D_FF), 0.02),
        'w_up': nrm(ks[29], (DEPTH, N_EXPERTS, D_MODEL, D_FF), D_MODEL ** -0.5),
        'b_up': nrm(ks[30], (DEPTH, N_EXPERTS, D_FF), 0.02),
        'w_down': nrm(ks[31], (DEPTH, N_EXPERTS, D_FF, D_MODEL), D_FF ** -0.5),
        'b_down': nrm(ks[32], (DEPTH, N_EXPERTS, D_MODEL), 0.02),
    }


def reference(x_prompt, x_sample, state_conv_a, state_conv_ssm, state_ssm, c_prompt, c_sample,
              norm1_w, norm2_w, final_norm_w, w_ada, b_ada, w_in, b_glu, w_dw, b_dw, ln_w, ln_b,
              w_xbc_conv, b_xbc_conv, dt_bias, a_log, d_skip, ssm_norm_w, w_out,
              w_router, b_router, w_gate, b_gate, w_up, b_up, w_down, b_down):
    p = dict(norm1_w=norm1_w, norm2_w=norm2_w, final_norm_w=final_norm_w, w_ada=w_ada, b_ada=b_ada,
             w_in=w_in, b_glu=b_glu, w_dw=w_dw, b_dw=b_dw, ln_w=ln_w, ln_b=ln_b,
             w_xbc_conv=w_xbc_conv, b_xbc_conv=b_xbc_conv, dt_bias=dt_bias, a_log=a_log,
             d_skip=d_skip, ssm_norm_w=ssm_norm_w, w_out=w_out, w_router=w_router,
             b_router=b_router, w_gate=w_gate, b_gate=b_gate, w_up=w_up, b_up=b_up,
             w_down=w_down, b_down=b_down)
    bsz = x_prompt.shape[0]
    dt_ = x_prompt.dtype
    zero_a = jnp.zeros((DEPTH, bsz, CONV_K - 1, CONV_CH), dt_)
    zero_b = jnp.zeros((DEPTH, bsz, SSM_CONV_K - 1, XBC_DIM), dt_)
    zero_s = jnp.zeros((DEPTH, bsz, SSM_HEADS, SSM_HEAD_DIM, D_STATE), dt_)
    y_prompt, conv_a_prompt, conv_ssm_prompt, ssm_prompt = run_group(x_prompt, c_prompt, zero_a, zero_b, zero_s, p)
    y_sample, conv_a_sample, conv_ssm_sample, ssm_sample = run_group(
        x_sample, c_sample, state_conv_a, state_conv_ssm, state_ssm, p)
    return (y_prompt, y_sample, conv_a_prompt, conv_ssm_prompt, ssm_prompt, conv_a_sample, conv_ssm_sample, ssm_sample)
```

```python
import functools

import jax
import jax.numpy as jnp
from jax import lax
from jax.experimental import pallas as pl
from jax.experimental.pallas import tpu as pltpu

F32 = jnp.float32
BF16 = jnp.bfloat16
I32 = jnp.int32

HEAD_DIM = 64
SSM_GROUPS = 8
D_STATE = 128
SSD_CHUNK = 128
TOP_K = 4
SWIGLU_LIMIT = 7.0
SWIGLU_ALPHA = 1.702
RMS_EPS = 1e-5
LN_EPS = 1e-5
HIST_ROWS = 32
VMEM_LIMIT = 56 * 1024 * 1024
MOE_TM = 256

_HI = lax.Precision.HIGHEST
_NT = (((1,), (1,)), ((), ()))
_TN = (((0,), (0,)), ((), ()))


def _cparams(n_axes):
    return pltpu.CompilerParams(dimension_semantics=("arbitrary",) * n_axes,
                                vmem_limit_bytes=VMEM_LIMIT)


def _silu(x):
    return x * jax.nn.sigmoid(x)


def _cast_rows(src_ref, dst_ref, chunk):
    rows = src_ref.shape[0]

    def body(c, carry):
        r = pl.multiple_of(c * chunk, chunk)
        dst_ref[pl.ds(r, chunk), :] = src_ref[pl.ds(r, chunk), :].astype(BF16)
        return carry

    lax.fori_loop(0, rows // chunk, body, 0)


def _mm_kernel(x_ref, w_ref, b_ref, o_ref, wb_ref, *, silu_in):
    @pl.when(pl.program_id(1) == 0)
    def _():
        _cast_rows(w_ref, wb_ref, 256)

    x = x_ref[...]
    if silu_in:
        x = _silu(x.astype(F32))
    o_ref[...] = jnp.dot(x.astype(BF16), wb_ref[...], preferred_element_type=F32) + b_ref[...]


def _mm(x, w, b, *, tm, tn, n_out, silu_in=False):
    m, k = x.shape
    return pl.pallas_call(
        functools.partial(_mm_kernel, silu_in=silu_in),
        grid=(n_out // tn, m // tm),
        in_specs=[pl.BlockSpec((tm, k), lambda j, i: (i, 0)),
                  pl.BlockSpec((k, tn), lambda j, i: (0, j)),
                  pl.BlockSpec((1, tn), lambda j, i: (0, j))],
        out_specs=pl.BlockSpec((tm, tn), lambda j, i: (i, j)),
        out_shape=jax.ShapeDtypeStruct((m, n_out), F32),
        scratch_shapes=[pltpu.VMEM((k, tn), BF16)],
        compiler_params=_cparams(2),
        name="dense_mm",
    )(x, w, b)


def _mod_operand(v, seq, tile):
    bsz, d = v.shape
    if seq % tile == 0:
        per = seq // tile
        arr = v.reshape(bsz, 1, d)

        def spec(tn, row_of, col_of):
            return pl.BlockSpec((None, 1, tn), lambda *g: (row_of(*g) // per, 0, col_of(*g)))
    else:
        arr = jnp.repeat(v, seq, axis=0)

        def spec(tn, row_of, col_of):
            return pl.BlockSpec((tile, tn), lambda *g: (row_of(*g), col_of(*g)))
    return arr, spec


def _norm_kernel(x_ref, w_ref, shift_ref, scale_ref, o_ref):
    x = x_ref[...]
    y = x * lax.rsqrt(jnp.mean(x * x, axis=-1, keepdims=True) + RMS_EPS) * w_ref[...]
    o_ref[...] = (y * (1.0 + scale_ref[...]) + shift_ref[...]).astype(o_ref.dtype)


def _norm_router_kernel(x_ref, w_ref, shift_ref, scale_ref, wr_ref, br_ref, o_ref, e_ref, p_ref):
    x = x_ref[...]
    y = x * lax.rsqrt(jnp.mean(x * x, axis=-1, keepdims=True) + RMS_EPS) * w_ref[...]
    hn = y * (1.0 + scale_ref[...]) + shift_ref[...]
    o_ref[...] = hn.astype(o_ref.dtype)
    vals = jnp.dot(hn, wr_ref[...], precision=_HI, preferred_element_type=F32) + br_ref[...]
    n_e = vals.shape[-1]
    lane = lax.broadcasted_iota(I32, vals.shape, 1)
    tops, idxs = [], []
    for _ in range(TOP_K):
        m = jnp.max(vals, axis=-1, keepdims=True)
        idx = jnp.min(jnp.where(vals == m, lane, n_e), axis=-1, keepdims=True)
        tops.append(m)
        idxs.append(idx)
        vals = jnp.where(lane == idx, -jnp.inf, vals)
    exps = [jnp.exp(t - tops[0]) for t in tops]
    denom = exps[0]
    for ex in exps[1:]:
        denom = denom + ex
    out_lane = lax.broadcasted_iota(I32, e_ref.shape, 1)
    e_out = jnp.zeros(e_ref.shape, I32)
    p_out = jnp.zeros(p_ref.shape, F32)
    for k in range(TOP_K):
        e_out = jnp.where(out_lane == k, idxs[k], e_out)
        p_out = jnp.where(out_lane == k, exps[k] / denom, p_out)
    e_ref[...] = e_out
    p_ref[...] = p_out


def _norm_mod(x2d, w, shift, scale, seq, *, tl, router=None):
    m, d = x2d.shape
    shift_a, shift_s = _mod_operand(shift, seq, tl)
    scale_a, scale_s = _mod_operand(scale, seq, tl)
    row_of = lambda i: i
    col_of = lambda i: 0
    in_specs = [pl.BlockSpec((tl, d), lambda i: (i, 0)),
                pl.BlockSpec((1, d), lambda i: (0, 0)),
                shift_s(d, row_of, col_of), scale_s(d, row_of, col_of)]
    args = [x2d, w.reshape(1, d), shift_a, scale_a]
    hn_spec = pl.BlockSpec((tl, d), lambda i: (i, 0))
    hn_shape = jax.ShapeDtypeStruct((m, d), BF16)
    if router is None:
        return pl.pallas_call(
            _norm_kernel, grid=(m // tl,), in_specs=in_specs, out_specs=hn_spec,
            out_shape=hn_shape, compiler_params=_cparams(1), name="ada_norm",
        )(*args)
    w_r, b_r = router
    n_e = w_r.shape[-1]
    in_specs += [pl.BlockSpec((d, n_e), lambda i: (0, 0)), pl.BlockSpec((1, n_e), lambda i: (0, 0))]
    args += [w_r, b_r.reshape(1, n_e)]
    return pl.pallas_call(
        _norm_router_kernel, grid=(m // tl,), in_specs=in_specs,
        out_specs=[hn_spec, pl.BlockSpec((tl, 128), lambda i: (i, 0)), pl.BlockSpec((tl, 128), lambda i: (i, 0))],
        out_shape=[hn_shape, jax.ShapeDtypeStruct((m, 128), I32), jax.ShapeDtypeStruct((m, 128), F32)],
        compiler_params=_cparams(1), name="ada_norm_router",
    )(*args)


def _conv_a_kernel(a_ref, g_ref, bglu_ref, hist_ref, wdw_ref, bdw_ref, lnw_ref, lnb_ref,
                   o_ref, newhist_ref, ubuf, vbuf, *, tl, width, cc):
    ch = a_ref.shape[-1]
    off = HIST_ROWS - (width - 1)

    @pl.when(pl.program_id(1) == 0)
    def _():
        ubuf[0:HIST_ROWS, :] = hist_ref[...]

    a = a_ref[...] + bglu_ref[:, 0:ch]
    g = g_ref[...] + bglu_ref[:, ch:2 * ch]
    ubuf[HIST_ROWS:HIST_ROWS + tl, :] = a * jax.nn.sigmoid(g)

    rc = min(tl, 128)
    for ri in range(tl // rc):
        r0 = ri * rc

        def col_body(ci, carry, r0=r0):
            c0 = pl.multiple_of(ci * cc, cc)
            acc = jnp.broadcast_to(bdw_ref[:, pl.ds(c0, cc)], (rc, cc))
            for k in range(width):
                acc = acc + wdw_ref[k:k + 1, pl.ds(c0, cc)] * ubuf[r0 + off + k:r0 + off + k + rc, pl.ds(c0, cc)]
            vbuf[r0:r0 + rc, pl.ds(c0, cc)] = acc
            return carry

        lax.fori_loop(0, ch // cc, col_body, 0)

        v = vbuf[r0:r0 + rc, :]
        mu = jnp.mean(v, axis=-1, keepdims=True)
        vc = v - mu
        var = jnp.mean(vc * vc, axis=-1, keepdims=True)
        y = vc * lax.rsqrt(var + LN_EPS) * lnw_ref[...] + lnb_ref[...]
        o_ref[r0:r0 + rc, :] = _silu(y).astype(o_ref.dtype)

    tail = ubuf[tl:tl + HIST_ROWS, :]
    newhist_ref[...] = tail
    ubuf[0:HIST_ROWS, :] = tail


def _conv_a(proj3, b_glu, hist, w_dw, b_dw, ln_w, ln_b, *, tl):
    bsz, seq, _ = proj3.shape
    width, ch = w_dw.shape
    cc = 256
    kern = functools.partial(_conv_a_kernel, tl=tl, width=width, cc=cc)
    full = lambda shape: pl.BlockSpec(shape, lambda b, t: (0,) * len(shape))
    return pl.pallas_call(
        kern, grid=(bsz, seq // tl),
        in_specs=[pl.BlockSpec((None, tl, ch), lambda b, t: (b, t, 0)),
                  pl.BlockSpec((None, tl, ch), lambda b, t: (b, t, 1)),
                  full((1, 2 * ch)),
                  pl.BlockSpec((None, HIST_ROWS, ch), lambda b, t: (b, 0, 0)),
                  full((width, ch)), full((1, ch)), full((1, ch)), full((1, ch))],
        out_specs=[pl.BlockSpec((None, tl, ch), lambda b, t: (b, t, 0)),
                   pl.BlockSpec((None, HIST_ROWS, ch), lambda b, t: (b, 0, 0))],
        out_shape=[jax.ShapeDtypeStruct((bsz, seq, ch), BF16),
                   jax.ShapeDtypeStruct((bsz, HIST_ROWS, ch), F32)],
        scratch_shapes=[pltpu.VMEM((HIST_ROWS + tl, ch), F32), pltpu.VMEM((tl, ch), F32)],
        compiler_params=_cparams(2), name="conformer_conv",
    )(proj3, proj3, b_glu.reshape(1, -1), hist, w_dw, b_dw.reshape(1, -1),
      ln_w.reshape(1, -1), ln_b.reshape(1, -1))


def _conv_b_kernel(xs_ref, bm_ref, cm_ref, dtr_ref, hx_ref, hb_ref, hc_ref, w_ref, b_ref, dtb_ref,
                   oxs_ref, obm_ref, ocm_ref, odt_ref, nhx_ref, nhb_ref, nhc_ref,
                   bx, bb, bc, *, tl, width, cc):
    off = HIST_ROWS - (width - 1)
    segs = ((xs_ref, hx_ref, oxs_ref, nhx_ref, bx), (bm_ref, hb_ref, obm_ref, nhb_ref, bb),
            (cm_ref, hc_ref, ocm_ref, nhc_ref, bc))
    first = pl.program_id(1) == 0
    rc = min(tl, 128)
    col0 = 0
    for in_ref, h_ref, out_ref, nh_ref, buf in segs:
        wseg = in_ref.shape[-1]

        @pl.when(first)
        def _(buf=buf, h_ref=h_ref):
            buf[0:HIST_ROWS, :] = h_ref[...]

        buf[HIST_ROWS:HIST_ROWS + tl, :] = in_ref[...]
        for ri in range(tl // rc):
            r0 = ri * rc

            def col_body(ci, carry, r0=r0, buf=buf, out_ref=out_ref, col0=col0):
                c0 = pl.multiple_of(ci * cc, cc)
                acc = jnp.broadcast_to(b_ref[:, pl.ds(col0 + c0, cc)], (rc, cc))
                for k in range(width):
                    acc = acc + w_ref[k:k + 1, pl.ds(col0 + c0, cc)] * buf[r0 + off + k:r0 + off + k + rc, pl.ds(c0, cc)]
                out_ref[r0:r0 + rc, pl.ds(c0, cc)] = _silu(acc)
                return carry

            lax.fori_loop(0, wseg // cc, col_body, 0)
        tail = buf[tl:tl + HIST_ROWS, :]
        nh_ref[...] = tail
        buf[0:HIST_ROWS, :] = tail
        col0 += wseg

    t = dtr_ref[...] + dtb_ref[...]
    odt_ref[...] = jnp.maximum(t, 0.0) + jnp.log1p(jnp.exp(-jnp.abs(t)))


def _conv_b(proj3, dt_raw3, hists, w_conv, b_conv, dt_bias, *, tl, d_ssm, col_xs):
    bsz, seq, _ = proj3.shape
    width, xbc = w_conv.shape
    gn = (xbc - d_ssm) // 2
    nh = dt_raw3.shape[-1]
    cc = 128
    kern = functools.partial(_conv_b_kernel, tl=tl, width=width, cc=cc)
    full = lambda shape: pl.BlockSpec(shape, lambda b, t: (0,) * len(shape))
    ix, ib, ic = col_xs // d_ssm, (col_xs + d_ssm) // gn, (col_xs + d_ssm + gn) // gn
    hspec = lambda w: pl.BlockSpec((None, HIST_ROWS, w), lambda b, t: (b, 0, 0))
    ospec = lambda w: pl.BlockSpec((None, tl, w), lambda b, t: (b, t, 0))
    return pl.pallas_call(
        kern, grid=(bsz, seq // tl),
        in_specs=[pl.BlockSpec((None, tl, d_ssm), lambda b, t: (b, t, ix)),
                  pl.BlockSpec((None, tl, gn), lambda b, t: (b, t, ib)),
                  pl.BlockSpec((None, tl, gn), lambda b, t: (b, t, ic)),
                  ospec(nh), hspec(d_ssm), hspec(gn), hspec(gn),
                  full((width, xbc)), full((1, xbc)), full((1, nh))],
        out_specs=[ospec(d_ssm), ospec(gn), ospec(gn), ospec(nh), hspec(d_ssm), hspec(gn), hspec(gn)],
        out_shape=[jax.ShapeDtypeStruct((bsz, seq, d_ssm), F32),
                   jax.ShapeDtypeStruct((bsz, seq, gn), F32),
                   jax.ShapeDtypeStruct((bsz, seq, gn), F32),
                   jax.ShapeDtypeStruct((bsz, seq, nh), F32),
                   jax.ShapeDtypeStruct((bsz, HIST_ROWS, d_ssm), F32),
                   jax.ShapeDtypeStruct((bsz, HIST_ROWS, gn), F32),
                   jax.ShapeDtypeStruct((bsz, HIST_ROWS, gn), F32)],
        scratch_shapes=[pltpu.VMEM((HIST_ROWS + tl, d_ssm), F32),
                        pltpu.VMEM((HIST_ROWS + tl, gn), F32),
                        pltpu.VMEM((HIST_ROWS + tl, gn), F32)],
        compiler_params=_cparams(2), name="ssm_conv",
    )(proj3, proj3, proj3, dt_raw3, *hists, w_conv, b_conv.reshape(1, -1), dt_bias.reshape(1, -1))


def _ssd_kernel(xs_ref, bm_ref, cm_ref, dt_ref, alog_ref, h0_ref, y_ref, hout_ref, h_scr, *, q, hpg):
    c = pl.program_id(2)

    @pl.when(c == 0)
    def _():
        h_scr[...] = h0_ref[...]

    p = HEAD_DIM
    x = xs_ref[...]
    bm = bm_ref[...]
    cm = cm_ref[...]
    bmb = bm.astype(BF16)
    cmb = cm.astype(BF16)
    dt = dt_ref[...]
    da = dt * (-jnp.exp(alog_ref[...]))
    row = lax.broadcasted_iota(I32, (q, q), 0)
    col = lax.broadcasted_iota(I32, (q, q), 1)
    causal = row >= col
    cum = jnp.dot(causal.astype(F32), da, precision=_HI, preferred_element_type=F32)
    cum_t = cum.T
    dt_t = dt.T
    cb = lax.dot_general(cmb, bmb, _NT, preferred_element_type=F32)
    cum_last = cum[q - 1:q, :]
    dd = jnp.exp(cum_last - cum) * dt
    ecum = jnp.exp(cum)
    cdec = jnp.exp(cum_last)
    for r in range(hpg):
        xr = x[:, r * p:(r + 1) * p]
        diff = cum[:, r:r + 1] - cum_t[r:r + 1, :]
        decay = jnp.exp(jnp.where(causal, diff, -jnp.inf))
        w = cb * decay * dt_t[r:r + 1, :]
        yd = jnp.dot(w.astype(BF16), xr.astype(BF16), preferred_element_type=F32)
        hin = h_scr[r]
        yo = lax.dot_general(cmb, hin.astype(BF16), _NT, preferred_element_type=F32) * ecum[:, r:r + 1]
        xdd = xr * dd[:, r:r + 1]
        st = lax.dot_general(xdd.astype(BF16), bmb, _TN, preferred_element_type=F32)
        h_scr[r] = cdec[:, r:r + 1] * hin + st
        y_ref[:, r * p:(r + 1) * p] = yd + yo

    @pl.when(c == pl.num_programs(2) - 1)
    def _():
        hout_ref[...] = h_scr[...]


def _ssd(xs_c, bm_c, cm_c, dt_g, a_log_g, h0, *, q):
    bsz, seq, d_ssm = xs_c.shape
    n_groups = dt_g.shape[1]
    hpg = dt_g.shape[-1]
    gw = hpg * HEAD_DIM
    n = D_STATE
    kern = functools.partial(_ssd_kernel, q=q, hpg=hpg)
    return pl.pallas_call(
        kern, grid=(bsz, n_groups, seq // q),
        in_specs=[pl.BlockSpec((None, q, gw), lambda b, g, c: (b, c, g)),
                  pl.BlockSpec((None, q, n), lambda b, g, c: (b, c, g)),
                  pl.BlockSpec((None, q, n), lambda b, g, c: (b, c, g)),
                  pl.BlockSpec((None, None, q, hpg), lambda b, g, c: (b, g, c, 0)),
                  pl.BlockSpec((None, 1, hpg), lambda b, g, c: (g, 0, 0)),
                  pl.BlockSpec((None, hpg, HEAD_DIM, n), lambda b, g, c: (b, g, 0, 0))],
        out_specs=[pl.BlockSpec((None, q, gw), lambda b, g, c: (b, c, g)),
                   pl.BlockSpec((None, hpg, HEAD_DIM, n), lambda b, g, c: (b, g, 0, 0))],
        out_shape=[jax.ShapeDtypeStruct((bsz, seq, d_ssm), F32),
                   jax.ShapeDtypeStruct(h0.shape, F32)],
        scratch_shapes=[pltpu.VMEM((hpg, HEAD_DIM, n), F32)],
        compiler_params=_cparams(3), name="ssd_scan",
    )(xs_c, bm_c, cm_c, dt_g, a_log_g, h0)


def _gate_norm_kernel(y_ref, xs_ref, z_ref, dsk_ref, nw_ref, o_ref, *, gw):
    y = y_ref[...] + xs_ref[...] * dsk_ref[...]
    yz = y * _silu(z_ref[...])
    for g in range(y.shape[-1] // gw):
        seg = yz[:, g * gw:(g + 1) * gw]
        seg = seg * lax.rsqrt(jnp.mean(seg * seg, axis=-1, keepdims=True) + RMS_EPS)
        o_ref[:, g * gw:(g + 1) * gw] = (seg * nw_ref[:, g * gw:(g + 1) * gw]).astype(o_ref.dtype)


def _gate_norm(y2, xs2, proj2, d_skip_lanes, norm_w, *, tl, col_z, gw):
    m, d_ssm = y2.shape
    row = pl.BlockSpec((tl, d_ssm), lambda i: (i, 0))
    return pl.pallas_call(
        functools.partial(_gate_norm_kernel, gw=gw), grid=(m // tl,),
        in_specs=[row, row, pl.BlockSpec((tl, d_ssm), lambda i: (i, col_z // d_ssm)),
                  pl.BlockSpec((1, d_ssm), lambda i: (0, 0)), pl.BlockSpec((1, d_ssm), lambda i: (0, 0))],
        out_specs=row, out_shape=jax.ShapeDtypeStruct((m, d_ssm), BF16),
        compiler_params=_cparams(1), name="ssm_gate_norm",
    )(y2, xs2, proj2, d_skip_lanes, norm_w.reshape(1, -1))


def _out_proj_kernel(xa_ref, xb_ref, w_ref, x_ref, gate_ref, o_ref, wb_ref):
    @pl.when(pl.program_id(1) == 0)
    def _():
        _cast_rows(w_ref, wb_ref, 256)

    ka = xa_ref.shape[-1]
    kb = xb_ref.shape[-1]
    mixed = jnp.dot(xa_ref[...], wb_ref[0:ka, :], preferred_element_type=F32)
    mixed = mixed + jnp.dot(xb_ref[...], wb_ref[ka:ka + kb, :], preferred_element_type=F32)
    o_ref[...] = x_ref[...] + gate_ref[...] * mixed


def _out_proj(xa, xb, w_out, x2d, gate, seq, *, tm, tn):
    m, d = x2d.shape
    ka, kb = xa.shape[-1], xb.shape[-1]
    gate_a, gate_s = _mod_operand(gate, seq, tm)
    return pl.pallas_call(
        _out_proj_kernel, grid=(d // tn, m // tm),
        in_specs=[pl.BlockSpec((tm, ka), lambda j, i: (i, 0)),
                  pl.BlockSpec((tm, kb), lambda j, i: (i, 0)),
                  pl.BlockSpec((ka + kb, tn), lambda j, i: (0, j)),
                  pl.BlockSpec((tm, tn), lambda j, i: (i, j)),
                  gate_s(tn, lambda j, i: i, lambda j, i: j)],
        out_specs=pl.BlockSpec((tm, tn), lambda j, i: (i, j)),
        out_shape=jax.ShapeDtypeStruct((m, d), F32),
        scratch_shapes=[pltpu.VMEM((ka + kb, tn), BF16)],
        compiler_params=_cparams(2), name="out_proj",
    )(xa, xb, w_out, x2d, gate_a)


def _expert_changed(be_ref, i):
    prev = be_ref[jnp.maximum(i - 1, 0)]
    return jnp.logical_or(i == 0, be_ref[i] != prev)


def _moe_up_kernel(be_ref, nu_ref, x_ref, wg_ref, wu_ref, bg_ref, bu_ref, o_ref, wgb, wub):
    i = pl.program_id(1)

    @pl.when(i < nu_ref[0])
    def _():
        @pl.when(_expert_changed(be_ref, i))
        def _():
            _cast_rows(wg_ref, wgb, 256)
            _cast_rows(wu_ref, wub, 256)

        x = x_ref[...]
        g = jnp.dot(x, wgb[...], preferred_element_type=F32) + bg_ref[...]
        u = jnp.dot(x, wub[...], preferred_element_type=F32) + bu_ref[...]
        g = jnp.minimum(g, SWIGLU_LIMIT)
        u = jnp.clip(u, -SWIGLU_LIMIT, SWIGLU_LIMIT)
        o_ref[...] = ((u + 1.0) * (g * jax.nn.sigmoid(SWIGLU_ALPHA * g))).astype(o_ref.dtype)


def _moe_down_kernel(be_ref, nu_ref, a_ref, wd_ref, bd_ref, o_ref, wdb):
    i = pl.program_id(1)

    @pl.when(i < nu_ref[0])
    def _():
        @pl.when(_expert_changed(be_ref, i))
        def _():
            _cast_rows(wd_ref, wdb, 256)

        o_ref[...] = jnp.dot(a_ref[...], wdb[...], preferred_element_type=F32) + bd_ref[...]


def _moe_ffn_rows(xg, blk_e, n_used, w_gate, b_gate, w_up, b_up, w_down, b_down, *, tf, tn):
    rows, d = xg.shape
    n_e, _, d_ff = w_gate.shape
    nb = rows // MOE_TM
    row_of = lambda j, i, be, nu: jnp.minimum(i, nu[0] - 1)
    act = pl.pallas_call(
        _moe_up_kernel,
        grid_spec=pltpu.PrefetchScalarGridSpec(
            num_scalar_prefetch=2, grid=(d_ff // tf, nb),
            in_specs=[pl.BlockSpec((MOE_TM, d), lambda j, i, be, nu: (row_of(j, i, be, nu), 0)),
                      pl.BlockSpec((None, d, tf), lambda j, i, be, nu: (be[i], 0, j)),
                      pl.BlockSpec((None, d, tf), lambda j, i, be, nu: (be[i], 0, j)),
                      pl.BlockSpec((None, 1, tf), lambda j, i, be, nu: (be[i], 0, j)),
                      pl.BlockSpec((None, 1, tf), lambda j, i, be, nu: (be[i], 0, j))],
            out_specs=pl.BlockSpec((MOE_TM, tf), lambda j, i, be, nu: (row_of(j, i, be, nu), j)),
            scratch_shapes=[pltpu.VMEM((d, tf), BF16), pltpu.VMEM((d, tf), BF16)]),
        out_shape=jax.ShapeDtypeStruct((rows, d_ff), BF16),
        compiler_params=_cparams(2), name="moe_gate_up",
    )(blk_e, n_used, xg, w_gate, w_up, b_gate.reshape(n_e, 1, d_ff), b_up.reshape(n_e, 1, d_ff))
    return pl.pallas_call(
        _moe_down_kernel,
        grid_spec=pltpu.PrefetchScalarGridSpec(
            num_scalar_prefetch=2, grid=(d // tn, nb),
            in_specs=[pl.BlockSpec((MOE_TM, d_ff), lambda j, i, be, nu: (row_of(j, i, be, nu), 0)),
                      pl.BlockSpec((None, d_ff, tn), lambda j, i, be, nu: (be[i], 0, j)),
                      pl.BlockSpec((None, 1, tn), lambda j, i, be, nu: (be[i], 0, j))],
            out_specs=pl.BlockSpec((MOE_TM, tn), lambda j, i, be, nu: (row_of(j, i, be, nu), j)),
            scratch_shapes=[pltpu.VMEM((d_ff, tn), BF16)]),
        out_shape=jax.ShapeDtypeStruct((rows, d), F32),
        compiler_params=_cparams(2), name="moe_down",
    )(blk_e, n_used, act, w_down, b_down.reshape(n_e, 1, d))


def _moe_dispatch(top_e, n_experts):
    n_tok = top_e.shape[0]
    n_assign = n_tok * TOP_K
    flat_e = top_e.reshape(-1)
    order = jnp.argsort(flat_e)
    sorted_e = flat_e[order]
    counts = jnp.bincount(flat_e, length=n_experts).astype(I32)
    padded = (counts + MOE_TM - 1) // MOE_TM * MOE_TM
    pad_end = jnp.cumsum(padded)
    pad_start = pad_end - padded
    start = jnp.cumsum(counts) - counts
    dest = pad_start[sorted_e] + jnp.arange(n_assign, dtype=I32) - start[sorted_e]
    nb = (n_assign + n_experts * (MOE_TM - 1) + MOE_TM - 1) // MOE_TM
    row_tok = jnp.zeros((nb * MOE_TM,), I32).at[dest].set((order // TOP_K).astype(I32))
    pos = jnp.zeros((n_assign,), I32).at[order].set(dest).reshape(n_tok, TOP_K)
    n_used = (pad_end[-1] // MOE_TM).astype(I32)
    blk = jnp.minimum(jnp.arange(nb, dtype=I32), n_used - 1) * MOE_TM
    blk_e = jnp.minimum(jnp.searchsorted(pad_end, blk, side="right"), n_experts - 1).astype(I32)
    return row_tok, pos, blk_e, n_used.reshape(1)


def _final_kernel(x_ref, f_ref, gate_ref, w_ref, o_ref):
    x = x_ref[...] + gate_ref[...] * f_ref[...]
    o_ref[...] = x * lax.rsqrt(jnp.mean(x * x, axis=-1, keepdims=True) + RMS_EPS) * w_ref[...]


def _final(x2d, ffn, gate, w, seq, *, tl):
    m, d = x2d.shape
    gate_a, gate_s = _mod_operand(gate, seq, tl)
    row = pl.BlockSpec((tl, d), lambda i: (i, 0))
    return pl.pallas_call(
        _final_kernel, grid=(m // tl,),
        in_specs=[row, row, gate_s(d, lambda i: i, lambda i: 0), pl.BlockSpec((1, d), lambda i: (0, 0))],
        out_specs=row, out_shape=jax.ShapeDtypeStruct((m, d), F32),
        compiler_params=_cparams(1), name="final_norm",
    )(x2d, ffn, gate_a, w.reshape(1, d))


def _pad_hist(buf):
    return jnp.pad(buf, ((0, 0), (HIST_ROWS - buf.shape[1], 0), (0, 0)))


def _group_front(x, mod, buf_a, buf_b, ssm_h0, p, *, tl, tr, tm):
    bsz, seq, d = x.shape
    m = bsz * seq
    shift1, scale1, gate1, shift2, scale2, _ = jnp.split(mod, 6, axis=-1)
    conv_ch = p["w_dw"].shape[-1]
    d_ssm = p["ssm_norm_w"].shape[-1]
    n_heads = d_ssm // HEAD_DIM
    hpg = n_heads // SSM_GROUPS
    xbc = p["w_xbc_conv"].shape[-1]
    n_main = 2 * conv_ch + d_ssm + xbc
    x2d = x.reshape(m, d)

    hn1 = _norm_mod(x2d, p["norm1_w"], shift1, scale1, seq, tl=tr)
    proj = _mm(hn1, p["w_in"], jnp.zeros((1, n_main), F32), tm=tm, tn=512, n_out=n_main)
    dt_raw = _mm(hn1, p["w_in"][:, n_main:], jnp.zeros((1, n_heads), F32), tm=tm, tn=n_heads, n_out=n_heads)
    proj3 = proj.reshape(bsz, seq, n_main)

    out_a, hist_a = _conv_a(proj3, p["b_glu"], _pad_hist(buf_a), p["w_dw"], p["b_dw"], p["ln_w"], p["ln_b"], tl=tl)
    new_a = hist_a[:, HIST_ROWS - buf_a.shape[1]:]

    gn = (xbc - d_ssm) // 2
    hists = (_pad_hist(buf_b[..., :d_ssm]), _pad_hist(buf_b[..., d_ssm:d_ssm + gn]), _pad_hist(buf_b[..., d_ssm + gn:]))
    xs_c, bm_c, cm_c, dt, nhx, nhb, nhc = _conv_b(
        proj3, dt_raw.reshape(bsz, seq, n_heads), hists, p["w_xbc_conv"], p["b_xbc_conv"], p["dt_bias"],
        tl=tl, d_ssm=d_ssm, col_xs=2 * conv_ch + d_ssm)
    kb = buf_b.shape[1]
    new_b = jnp.concatenate([nhx, nhb, nhc], axis=-1)[:, HIST_ROWS - kb:]

    q = SSD_CHUNK if seq % SSD_CHUNK == 0 else seq
    dt_g = dt.reshape(bsz, seq, SSM_GROUPS, hpg).transpose(0, 2, 1, 3)
    y, h_fin = _ssd(xs_c, bm_c, cm_c, dt_g, p["a_log"].reshape(SSM_GROUPS, 1, hpg), ssm_h0, q=q)
    out_b = _gate_norm(y.reshape(m, d_ssm), xs_c.reshape(m, d_ssm), proj,
                       jnp.repeat(p["d_skip"], HEAD_DIM).reshape(1, d_ssm), p["ssm_norm_w"],
                       tl=tr, col_z=2 * conv_ch, gw=d_ssm // SSM_GROUPS)

    x1 = _out_proj(out_a.reshape(m, conv_ch), out_b, p["w_out"], x2d, gate1, seq, tm=tm, tn=512)
    hn2, top_e, probs = _norm_mod(x1, p["norm2_w"], shift2, scale2, seq, tl=tr,
                                  router=(p["w_router"], p["b_router"]))
    return x1, hn2, top_e[:, :TOP_K], probs[:, :TOP_K], new_a, new_b, h_fin


def kernel(x_prompt, x_sample, state_conv_a, state_conv_ssm, state_ssm, c_prompt, c_sample, norm1_w, norm2_w, final_norm_w, w_ada, b_ada, w_in, b_glu, w_dw, b_dw, ln_w, ln_b, w_xbc_conv, b_xbc_conv, dt_bias, a_log, d_skip, ssm_norm_w, w_out, w_router, b_router, w_gate, b_gate, w_up, b_up, w_down, b_down):
    layer = 0
    p = dict(norm1_w=norm1_w[layer], norm2_w=norm2_w[layer], w_in=w_in[layer], b_glu=b_glu[layer],
             w_dw=w_dw[layer], b_dw=b_dw[layer], ln_w=ln_w[layer], ln_b=ln_b[layer],
             w_xbc_conv=w_xbc_conv[layer], b_xbc_conv=b_xbc_conv[layer], dt_bias=dt_bias[layer],
             a_log=a_log[layer], d_skip=d_skip[layer], ssm_norm_w=ssm_norm_w[layer], w_out=w_out[layer],
             w_router=w_router[layer], b_router=b_router[layer])
    bp, sp, d = x_prompt.shape
    bs, ss, _ = x_sample.shape
    mp, ms = bp * sp, bs * ss
    n_experts = w_router.shape[-1]

    c_all = jnp.concatenate([c_prompt, c_sample], axis=0)
    n_c = c_all.shape[0]
    n_cp = (n_c + 7) // 8 * 8
    c_all = jnp.pad(c_all, ((0, n_cp - n_c), (0, 0)))
    mod = _mm(c_all, w_ada[layer], b_ada[layer].reshape(1, -1), tm=n_cp, tn=512, n_out=w_ada.shape[-1], silu_in=True)

    zeros = lambda a, b: jnp.zeros((b,) + a.shape[2:], a.dtype)
    x1p, hn2p, ep, pp, conv_a_p, conv_b_p, ssm_p = _group_front(
        x_prompt, mod[:bp], zeros(state_conv_a, bp), zeros(state_conv_ssm, bp), zeros(state_ssm, bp), p,
        tl=min(256, sp), tr=min(256, mp), tm=min(512, mp))
    x1s, hn2s, es, ps, conv_a_s, conv_b_s, ssm_s = _group_front(
        x_sample, mod[bp:bp + bs], state_conv_a[layer], state_conv_ssm[layer], state_ssm[layer], p,
        tl=ss, tr=min(256, ms), tm=min(512, ms))

    hn2 = jnp.concatenate([hn2p, hn2s], axis=0)
    top_e = jnp.concatenate([ep, es], axis=0)
    probs = jnp.concatenate([pp, ps], axis=0)
    row_tok, pos, blk_e, n_used = _moe_dispatch(top_e, n_experts)
    xg = jnp.take(hn2, row_tok, axis=0)
    rows_out = _moe_ffn_rows(xg, blk_e, n_used, w_gate[layer], b_gate[layer], w_up[layer], b_up[layer],
                             w_down[layer], b_down[layer], tf=512, tn=512)
    ffn = jnp.sum(jnp.take(rows_out, pos, axis=0) * probs[..., None], axis=1)

    gate2_p = mod[:bp, 5 * d:]
    gate2_s = mod[bp:bp + bs, 5 * d:]
    y_p = _final(x1p, ffn[:mp], gate2_p, final_norm_w, sp, tl=min(256, mp)).reshape(bp, sp, d)
    y_s = _final(x1s, ffn[mp:], gate2_s, final_norm_w, ss, tl=min(256, ms)).reshape(bs, ss, d)
    return (y_p, y_s, conv_a_p[None], conv_b_p[None], ssm_p[None], conv_a_s[None], conv_b_s[None], ssm_s[None])
```

```python
import functools

import jax
import jax.numpy as jnp
from jax import lax
from jax.experimental import pallas as pl
from jax.experimental.pallas import tpu as pltpu

F32 = jnp.float32
BF16 = jnp.bfloat16
I32 = jnp.int32

HEAD_DIM = 64
SSM_GROUPS = 8
D_STATE = 128
SSD_CHUNK = 128
TOP_K = 4
SWIGLU_LIMIT = 7.0
SWIGLU_ALPHA = 1.702
RMS_EPS = 1e-5
LN_EPS = 1e-5
HIST_ROWS = 32
VMEM_LIMIT = 56 * 1024 * 1024
MOE_TM = 256

_HI = lax.Precision.HIGHEST
_NT = (((1,), (1,)), ((), ()))
_TN = (((0,), (0,)), ((), ()))


def _cparams(n_axes):
    return pltpu.CompilerParams(dimension_semantics=("arbitrary",) * n_axes,
                                vmem_limit_bytes=VMEM_LIMIT)


def _silu(x):
    return x * jax.nn.sigmoid(x)


def _cast_rows(src_ref, dst_ref, chunk):
    rows = src_ref.shape[0]

    def body(c, carry):
        r = pl.multiple_of(c * chunk, chunk)
        dst_ref[pl.ds(r, chunk), :] = src_ref[pl.ds(r, chunk), :].astype(BF16)
        return carry

    lax.fori_loop(0, rows // chunk, body, 0)


def _mm_kernel(x_ref, w_ref, b_ref, o_ref, wb_ref, *, silu_in):
    @pl.when(pl.program_id(1) == 0)
    def _():
        _cast_rows(w_ref, wb_ref, 256)

    x = x_ref[...]
    if silu_in:
        x = _silu(x.astype(F32))
    o_ref[...] = jnp.dot(x.astype(BF16), wb_ref[...], preferred_element_type=F32) + b_ref[...]


def _mm(x, w, b, *, tm, tn, n_out, silu_in=False):
    m, k = x.shape
    return pl.pallas_call(
        functools.partial(_mm_kernel, silu_in=silu_in),
        grid=(n_out // tn, m // tm),
        in_specs=[pl.BlockSpec((tm, k), lambda j, i: (i, 0)),
                  pl.BlockSpec((k, tn), lambda j, i: (0, j)),
                  pl.BlockSpec((1, tn), lambda j, i: (0, j))],
        out_specs=pl.BlockSpec((tm, tn), lambda j, i: (i, j)),
        out_shape=jax.ShapeDtypeStruct((m, n_out), F32),
        scratch_shapes=[pltpu.VMEM((k, tn), BF16)],
        compiler_params=_cparams(2),
        name="dense_mm",
    )(x, w, b)


def _mod_operand(v, seq, tile):
    bsz, d = v.shape
    if seq % tile == 0:
        per = seq // tile
        arr = v.reshape(bsz, 1, d)

        def spec(tn, row_of, col_of):
            return pl.BlockSpec((None, 1, tn), lambda *g: (row_of(*g) // per, 0, col_of(*g)))
    else:
        arr = jnp.repeat(v, seq, axis=0)

        def spec(tn, row_of, col_of):
            return pl.BlockSpec((tile, tn), lambda *g: (row_of(*g), col_of(*g)))
    return arr, spec


def _norm_kernel(x_ref, w_ref, shift_ref, scale_ref, o_ref):
    x = x_ref[...]
    y = x * lax.rsqrt(jnp.mean(x * x, axis=-1, keepdims=True) + RMS_EPS) * w_ref[...]
    o_ref[...] = (y * (1.0 + scale_ref[...]) + shift_ref[...]).astype(o_ref.dtype)


def _norm_router_kernel(x_ref, w_ref, shift_ref, scale_ref, wr_ref, br_ref, o_ref, e_ref, p_ref):
    x = x_ref[...]
    y = x * lax.rsqrt(jnp.mean(x * x, axis=-1, keepdims=True) + RMS_EPS) * w_ref[...]
    hn = y * (1.0 + scale_ref[...]) + shift_ref[...]
    o_ref[...] = hn.astype(o_ref.dtype)
    vals = jnp.dot(hn, wr_ref[...], precision=_HI, preferred_element_type=F32) + br_ref[...]
    n_e = vals.shape[-1]
    lane = lax.broadcasted_iota(I32, vals.shape, 1)
    tops, idxs = [], []
    for _ in range(TOP_K):
        m = jnp.max(vals, axis=-1, keepdims=True)
        idx = jnp.min(jnp.where(vals == m, lane, n_e), axis=-1, keepdims=True)
        tops.append(m)
        idxs.append(idx)
        vals = jnp.where(lane == idx, -jnp.inf, vals)
    exps = [jnp.exp(t - tops[0]) for t in tops]
    denom = exps[0]
    for ex in exps[1:]:
        denom = denom + ex
    out_lane = lax.broadcasted_iota(I32, e_ref.shape, 1)
    e_out = jnp.zeros(e_ref.shape, I32)
    p_out = jnp.zeros(p_ref.shape, F32)
    for k in range(TOP_K):
        e_out = jnp.where(out_lane == k, idxs[k], e_out)
        p_out = jnp.where(out_lane == k, exps[k] / denom, p_out)
    e_ref[...] = e_out
    p_ref[...] = p_out


def _norm_mod(x2d, w, shift, scale, seq, *, tl, router=None):
    m, d = x2d.shape
    shift_a, shift_s = _mod_operand(shift, seq, tl)
    scale_a, scale_s = _mod_operand(scale, seq, tl)
    row_of = lambda i: i
    col_of = lambda i: 0
    in_specs = [pl.BlockSpec((tl, d), lambda i: (i, 0)),
                pl.BlockSpec((1, d), lambda i: (0, 0)),
                shift_s(d, row_of, col_of), scale_s(d, row_of, col_of)]
    args = [x2d, w.reshape(1, d), shift_a, scale_a]
    hn_spec = pl.BlockSpec((tl, d), lambda i: (i, 0))
    hn_shape = jax.ShapeDtypeStruct((m, d), BF16)
    if router is None:
        return pl.pallas_call(
            _norm_kernel, grid=(m // tl,), in_specs=in_specs, out_specs=hn_spec,
            out_shape=hn_shape, compiler_params=_cparams(1), name="ada_norm",
        )(*args)
    w_r, b_r = router
    n_e = w_r.shape[-1]
    in_specs += [pl.BlockSpec((d, n_e), lambda i: (0, 0)), pl.BlockSpec((1, n_e), lambda i: (0, 0))]
    args += [w_r, b_r.reshape(1, n_e)]
    return pl.pallas_call(
        _norm_router_kernel, grid=(m // tl,), in_specs=in_specs,
        out_specs=[hn_spec, pl.BlockSpec((tl, 128), lambda i: (i, 0)), pl.BlockSpec((tl, 128), lambda i: (i, 0))],
        out_shape=[hn_shape, jax.ShapeDtypeStruct((m, 128), I32), jax.ShapeDtypeStruct((m, 128), F32)],
        compiler_params=_cparams(1), name="ada_norm_router",
    )(*args)


def _conv_a_kernel(a_ref, g_ref, bglu_ref, hist_ref, wdw_ref, bdw_ref, lnw_ref, lnb_ref,
                   o_ref, newhist_ref, ubuf, vbuf, *, tl, width, cc):
    ch = a_ref.shape[-1]
    off = HIST_ROWS - (width - 1)

    @pl.when(pl.program_id(1) == 0)
    def _():
        ubuf[0:HIST_ROWS, :] = hist_ref[...]

    a = a_ref[...] + bglu_ref[:, 0:ch]
    g = g_ref[...] + bglu_ref[:, ch:2 * ch]
    ubuf[HIST_ROWS:HIST_ROWS + tl, :] = a * jax.nn.sigmoid(g)

    rc = min(tl, 128)
    for ri in range(tl // rc):
        r0 = ri * rc

        def col_body(ci, carry, r0=r0):
            c0 = pl.multiple_of(ci * cc, cc)
            acc = jnp.broadcast_to(bdw_ref[:, pl.ds(c0, cc)], (rc, cc))
            for k in range(width):
                acc = acc + wdw_ref[k:k + 1, pl.ds(c0, cc)] * ubuf[r0 + off + k:r0 + off + k + rc, pl.ds(c0, cc)]
            vbuf[r0:r0 + rc, pl.ds(c0, cc)] = acc
            return carry

        lax.fori_loop(0, ch // cc, col_body, 0)

        v = vbuf[r0:r0 + rc, :]
        mu = jnp.mean(v, axis=-1, keepdims=True)
        vc = v - mu
        var = jnp.mean(vc * vc, axis=-1, keepdims=True)
        y = vc * lax.rsqrt(var + LN_EPS) * lnw_ref[...] + lnb_ref[...]
        o_ref[r0:r0 + rc, :] = _silu(y).astype(o_ref.dtype)

    tail = ubuf[tl:tl + HIST_ROWS, :]
    newhist_ref[...] = tail
    ubuf[0:HIST_ROWS, :] = tail


def _conv_a(proj3, b_glu, hist, w_dw, b_dw, ln_w, ln_b, *, tl):
    bsz, seq, _ = proj3.shape
    width, ch = w_dw.shape
    cc = 256
    kern = functools.partial(_conv_a_kernel, tl=tl, width=width, cc=cc)
    full = lambda shape: pl.BlockSpec(shape, lambda b, t: (0,) * len(shape))
    return pl.pallas_call(
        kern, grid=(bsz, seq // tl),
        in_specs=[pl.BlockSpec((None, tl, ch), lambda b, t: (b, t, 0)),
                  pl.BlockSpec((None, tl, ch), lambda b, t: (b, t, 1)),
                  full((1, 2 * ch)),
                  pl.BlockSpec((None, HIST_ROWS, ch), lambda b, t: (b, 0, 0)),
                  full((width, ch)), full((1, ch)), full((1, ch)), full((1, ch))],
        out_specs=[pl.BlockSpec((None, tl, ch), lambda b, t: (b, t, 0)),
                   pl.BlockSpec((None, HIST_ROWS, ch), lambda b, t: (b, 0, 0))],
        out_shape=[jax.ShapeDtypeStruct((bsz, seq, ch), BF16),
                   jax.ShapeDtypeStruct((bsz, HIST_ROWS, ch), F32)],
        scratch_shapes=[pltpu.VMEM((HIST_ROWS + tl, ch), F32), pltpu.VMEM((tl, ch), F32)],
        compiler_params=_cparams(2), name="conformer_conv",
    )(proj3, proj3, b_glu.reshape(1, -1), hist, w_dw, b_dw.reshape(1, -1),
      ln_w.reshape(1, -1), ln_b.reshape(1, -1))


def _conv_b_kernel(xs_ref, bm_ref, cm_ref, dtr_ref, hx_ref, hb_ref, hc_ref, w_ref, b_ref, dtb_ref,
                   oxs_ref, obm_ref, ocm_ref, odt_ref, nhx_ref, nhb_ref, nhc_ref,
                   bx, bb, bc, *, tl, width, cc):
    off = HIST_ROWS - (width - 1)
    segs = ((xs_ref, hx_ref, oxs_ref, nhx_ref, bx), (bm_ref, hb_ref, obm_ref, nhb_ref, bb),
            (cm_ref, hc_ref, ocm_ref, nhc_ref, bc))
    first = pl.program_id(1) == 0
    rc = min(tl, 128)
    col0 = 0
    for in_ref, h_ref, out_ref, nh_ref, buf in segs:
        wseg = in_ref.shape[-1]

        @pl.when(first)
        def _(buf=buf, h_ref=h_ref):
            buf[0:HIST_ROWS, :] = h_ref[...]

        buf[HIST_ROWS:HIST_ROWS + tl, :] = in_ref[...]
        for ri in range(tl // rc):
            r0 = ri * rc

            def col_body(ci, carry, r0=r0, buf=buf, out_ref=out_ref, col0=col0):
                c0 = pl.multiple_of(ci * cc, cc)
                acc = jnp.broadcast_to(b_ref[:, pl.ds(col0 + c0, cc)], (rc, cc))
                for k in range(width):
                    acc = acc + w_ref[k:k + 1, pl.ds(col0 + c0, cc)] * buf[r0 + off + k:r0 + off + k + rc, pl.ds(c0, cc)]
                out_ref[r0:r0 + rc, pl.ds(c0, cc)] = _silu(acc)
                return carry

            lax.fori_loop(0, wseg // cc, col_body, 0)
        tail = buf[tl:tl + HIST_ROWS, :]
        nh_ref[...] = tail
        buf[0:HIST_ROWS, :] = tail
        col0 += wseg

    t = dtr_ref[...] + dtb_ref[...]
    odt_ref[...] = jnp.maximum(t, 0.0) + jnp.log1p(jnp.exp(-jnp.abs(t)))


def _conv_b(proj3, dt_raw3, hists, w_conv, b_conv, dt_bias, *, tl, d_ssm, col_xs):
    bsz, seq, _ = proj3.shape
    width, xbc = w_conv.shape
    gn = (xbc - d_ssm) // 2
    nh = dt_raw3.shape[-1]
    cc = 128
    kern = functools.partial(_conv_b_kernel, tl=tl, width=width, cc=cc)
    full = lambda shape: pl.BlockSpec(shape, lambda b, t: (0,) * len(shape))
    ix, ib, ic = col_xs // d_ssm, (col_xs + d_ssm) // gn, (col_xs + d_ssm + gn) // gn
    hspec = lambda w: pl.BlockSpec((None, HIST_ROWS, w), lambda b, t: (b, 0, 0))
    ospec = lambda w: pl.BlockSpec((None, tl, w), lambda b, t: (b, t, 0))
    return pl.pallas_call(
        kern, grid=(bsz, seq // tl),
        in_specs=[pl.BlockSpec((None, tl, d_ssm), lambda b, t: (b, t, ix)),
                  pl.BlockSpec((None, tl, gn), lambda b, t: (b, t, ib)),
                  pl.BlockSpec((None, tl, gn), lambda b, t: (b, t, ic)),
                  ospec(nh), hspec(d_ssm), hspec(gn), hspec(gn),
                  full((width, xbc)), full((1, xbc)), full((1, nh))],
        out_specs=[ospec(d_ssm), ospec(gn), ospec(gn), ospec(nh), hspec(d_ssm), hspec(gn), hspec(gn)],
        out_shape=[jax.ShapeDtypeStruct((bsz, seq, d_ssm), F32),
                   jax.ShapeDtypeStruct((bsz, seq, gn), F32),
                   jax.ShapeDtypeStruct((bsz, seq, gn), F32),
                   jax.ShapeDtypeStruct((bsz, seq, nh), F32),
                   jax.ShapeDtypeStruct((bsz, HIST_ROWS, d_ssm), F32),
                   jax.ShapeDtypeStruct((bsz, HIST_ROWS, gn), F32),
                   jax.ShapeDtypeStruct((bsz, HIST_ROWS, gn), F32)],
        scratch_shapes=[pltpu.VMEM((HIST_ROWS + tl, d_ssm), F32),
                        pltpu.VMEM((HIST_ROWS + tl, gn), F32),
                        pltpu.VMEM((HIST_ROWS + tl, gn), F32)],
        compiler_params=_cparams(2), name="ssm_conv",
    )(proj3, proj3, proj3, dt_raw3, *hists, w_conv, b_conv.reshape(1, -1), dt_bias.reshape(1, -1))


def _ssd_kernel(xs_ref, bm_ref, cm_ref, dt_ref, alog_ref, h0_ref, y_ref, hout_ref, h_scr, *, q, hpg, gpb):
    c = pl.program_id(2)

    @pl.when(c == 0)
    def _():
        h_scr[...] = h0_ref[...]

    p = HEAD_DIM
    n = D_STATE
    gw = hpg * p
    row = lax.broadcasted_iota(I32, (q, q), 0)
    col = lax.broadcasted_iota(I32, (q, q), 1)
    causal = row >= col
    tril = causal.astype(F32)
    for gi in range(gpb):
        bmb = bm_ref[:, gi * n:(gi + 1) * n].astype(BF16)
        cmb = cm_ref[:, gi * n:(gi + 1) * n].astype(BF16)
        dt = dt_ref[gi]
        da = dt * (-jnp.exp(alog_ref[gi]))
        cum = jnp.dot(tril, da, precision=_HI, preferred_element_type=F32)
        cum_t = cum.T
        dt_t = dt.T
        cb = lax.dot_general(cmb, bmb, _NT, preferred_element_type=F32)
        cum_last = cum[q - 1:q, :]
        dd = jnp.exp(cum_last - cum) * dt
        ecum = jnp.exp(cum)
        cdec = jnp.exp(cum_last)
        for r in range(hpg):
            c0 = gi * gw + r * p
            xr = xs_ref[:, c0:c0 + p]
            diff = cum[:, r:r + 1] - cum_t[r:r + 1, :]
            decay = jnp.exp(jnp.where(causal, diff, -jnp.inf))
            w = cb * decay * dt_t[r:r + 1, :]
            yd = jnp.dot(w.astype(BF16), xr.astype(BF16), preferred_element_type=F32)
            hin = h_scr[gi * hpg + r]
            yo = lax.dot_general(cmb, hin.astype(BF16), _NT, preferred_element_type=F32) * ecum[:, r:r + 1]
            xdd = xr * dd[:, r:r + 1]
            st = lax.dot_general(xdd.astype(BF16), bmb, _TN, preferred_element_type=F32)
            h_scr[gi * hpg + r] = cdec[:, r:r + 1] * hin + st
            y_ref[:, c0:c0 + p] = yd + yo

    @pl.when(c == pl.num_programs(2) - 1)
    def _():
        hout_ref[...] = h_scr[...]


def _ssd(xs_c, bm_c, cm_c, dt_g, a_log_g, h0, *, q, gpb):
    bsz, seq, d_ssm = xs_c.shape
    n_groups = dt_g.shape[1]
    hpg = dt_g.shape[-1]
    gw = hpg * HEAD_DIM
    n = D_STATE
    kern = functools.partial(_ssd_kernel, q=q, hpg=hpg, gpb=gpb)
    return pl.pallas_call(
        kern, grid=(bsz, n_groups // gpb, seq // q),
        in_specs=[pl.BlockSpec((None, q, gpb * gw), lambda b, g, c: (b, c, g)),
                  pl.BlockSpec((None, q, gpb * n), lambda b, g, c: (b, c, g)),
                  pl.BlockSpec((None, q, gpb * n), lambda b, g, c: (b, c, g)),
                  pl.BlockSpec((None, gpb, q, hpg), lambda b, g, c: (b, g, c, 0)),
                  pl.BlockSpec((gpb, 1, hpg), lambda b, g, c: (g, 0, 0)),
                  pl.BlockSpec((None, gpb * hpg, HEAD_DIM, n), lambda b, g, c: (b, g, 0, 0))],
        out_specs=[pl.BlockSpec((None, q, gpb * gw), lambda b, g, c: (b, c, g)),
                   pl.BlockSpec((None, gpb * hpg, HEAD_DIM, n), lambda b, g, c: (b, g, 0, 0))],
        out_shape=[jax.ShapeDtypeStruct((bsz, seq, d_ssm), F32),
                   jax.ShapeDtypeStruct(h0.shape, F32)],
        scratch_shapes=[pltpu.VMEM((gpb * hpg, HEAD_DIM, n), F32)],
        compiler_params=_cparams(3), name="ssd_scan",
    )(xs_c, bm_c, cm_c, dt_g, a_log_g, h0)


def _gate_norm_kernel(y_ref, xs_ref, z_ref, dsk_ref, nw_ref, o_ref, *, gw):
    y = y_ref[...] + xs_ref[...] * dsk_ref[...]
    yz = y * _silu(z_ref[...])
    for g in range(y.shape[-1] // gw):
        seg = yz[:, g * gw:(g + 1) * gw]
        seg = seg * lax.rsqrt(jnp.mean(seg * seg, axis=-1, keepdims=True) + RMS_EPS)
        o_ref[:, g * gw:(g + 1) * gw] = (seg * nw_ref[:, g * gw:(g + 1) * gw]).astype(o_ref.dtype)


def _gate_norm(y2, xs2, proj2, d_skip_lanes, norm_w, *, tl, col_z, gw):
    m, d_ssm = y2.shape
    row = pl.BlockSpec((tl, d_ssm), lambda i: (i, 0))
    return pl.pallas_call(
        functools.partial(_gate_norm_kernel, gw=gw), grid=(m // tl,),
        in_specs=[row, row, pl.BlockSpec((tl, d_ssm), lambda i: (i, col_z // d_ssm)),
                  pl.BlockSpec((1, d_ssm), lambda i: (0, 0)), pl.BlockSpec((1, d_ssm), lambda i: (0, 0))],
        out_specs=row, out_shape=jax.ShapeDtypeStruct((m, d_ssm), BF16),
        compiler_params=_cparams(1), name="ssm_gate_norm",
    )(y2, xs2, proj2, d_skip_lanes, norm_w.reshape(1, -1))


def _out_proj_kernel(xa_ref, xb_ref, w_ref, x_ref, gate_ref, o_ref, wb_ref):
    @pl.when(pl.program_id(1) == 0)
    def _():
        _cast_rows(w_ref, wb_ref, 256)

    ka = xa_ref.shape[-1]
    kb = xb_ref.shape[-1]
    mixed = jnp.dot(xa_ref[...], wb_ref[0:ka, :], preferred_element_type=F32)
    mixed = mixed + jnp.dot(xb_ref[...], wb_ref[ka:ka + kb, :], preferred_element_type=F32)
    o_ref[...] = x_ref[...] + gate_ref[...] * mixed


def _out_proj(xa, xb, w_out, x2d, gate, seq, *, tm, tn):
    m, d = x2d.shape
    ka, kb = xa.shape[-1], xb.shape[-1]
    gate_a, gate_s = _mod_operand(gate, seq, tm)
    return pl.pallas_call(
        _out_proj_kernel, grid=(d // tn, m // tm),
        in_specs=[pl.BlockSpec((tm, ka), lambda j, i: (i, 0)),
                  pl.BlockSpec((tm, kb), lambda j, i: (i, 0)),
                  pl.BlockSpec((ka + kb, tn), lambda j, i: (0, j)),
                  pl.BlockSpec((tm, tn), lambda j, i: (i, j)),
                  gate_s(tn, lambda j, i: i, lambda j, i: j)],
        out_specs=pl.BlockSpec((tm, tn), lambda j, i: (i, j)),
        out_shape=jax.ShapeDtypeStruct((m, d), F32),
        scratch_shapes=[pltpu.VMEM((ka + kb, tn), BF16)],
        compiler_params=_cparams(2), name="out_proj",
    )(xa, xb, w_out, x2d, gate_a)


N_SEG_META = 7


def _moe_kernel(be_ref, nu_ref, first_ref, nxt_ref, lastseg_ref, segidx_ref, nseg_ref, x_ref, *rest, n_w, swiglu):
    w_hbm = rest[:n_w]
    b_refs = rest[n_w:2 * n_w]
    o_ref = rest[2 * n_w]
    wbuf, wb, sem = rest[2 * n_w + 1:]
    j = pl.program_id(0)
    i = pl.program_id(1)
    n_j = pl.num_programs(0)
    tf = wb.shape[-1]

    def w_copy(t, e, jj, slot):
        cols = pl.ds(pl.multiple_of(jj * tf, tf), tf)
        return pltpu.make_async_copy(w_hbm[t].at[e, :, cols], wbuf.at[slot, t], sem.at[slot, t])

    @pl.when(i < nu_ref[0])
    def _():
        @pl.when(first_ref[i] == 1)
        def _():
            seg = j * nseg_ref[0] + segidx_ref[i]
            slot = lax.rem(seg, 2)
            e = be_ref[i]

            @pl.when(seg == 0)
            def _():
                for t in range(n_w):
                    w_copy(t, e, j, slot).start()

            last = lastseg_ref[i]

            @pl.when(jnp.logical_not(jnp.logical_and(last == 1, j == n_j - 1)))
            def _():
                for t in range(n_w):
                    w_copy(t, nxt_ref[i], j + last, 1 - slot).start()

            for t in range(n_w):
                w_copy(t, e, j, slot).wait()
                _cast_rows(wbuf.at[slot, t], wb.at[t], 256)

        x = x_ref[...]
        if swiglu:
            g = jnp.dot(x, wb[0], preferred_element_type=F32) + b_refs[0][...]
            u = jnp.dot(x, wb[1], preferred_element_type=F32) + b_refs[1][...]
            g = jnp.minimum(g, SWIGLU_LIMIT)
            u = jnp.clip(u, -SWIGLU_LIMIT, SWIGLU_LIMIT)
            o_ref[...] = ((u + 1.0) * (g * jax.nn.sigmoid(SWIGLU_ALPHA * g))).astype(o_ref.dtype)
        else:
            o_ref[...] = jnp.dot(x, wb[0], preferred_element_type=F32) + b_refs[0][...]

    @pl.when(i >= nu_ref[0])
    def _():
        o_ref[...] = jnp.zeros(o_ref.shape, o_ref.dtype)


def _moe_call(x, meta, weights, biases, *, tn, swiglu, out_dtype, name):
    rows, k = x.shape
    n_e, _, n_out = weights[0].shape
    n_w = len(weights)
    nb = rows // MOE_TM
    row_of = lambda j, i, be, nu, *_: jnp.minimum(i, nu[0] - 1)
    bias_spec = pl.BlockSpec((None, 1, tn), lambda j, i, be, *_: (be[i], 0, j))
    return pl.pallas_call(
        functools.partial(_moe_kernel, n_w=n_w, swiglu=swiglu),
        grid_spec=pltpu.PrefetchScalarGridSpec(
            num_scalar_prefetch=N_SEG_META, grid=(n_out // tn, nb),
            in_specs=[pl.BlockSpec((MOE_TM, k), lambda j, i, *s: (row_of(j, i, *s), 0))]
            + [pl.BlockSpec(memory_space=pl.ANY)] * n_w + [bias_spec] * n_w,
            out_specs=pl.BlockSpec((MOE_TM, tn), lambda j, i, *s: (i, j)),
            scratch_shapes=[pltpu.VMEM((2, n_w, k, tn), F32), pltpu.VMEM((n_w, k, tn), BF16),
                            pltpu.SemaphoreType.DMA((2, n_w))]),
        out_shape=jax.ShapeDtypeStruct((rows, n_out), out_dtype),
        compiler_params=_cparams(2), name=name,
    )(*meta, x, *weights, *[b.reshape(n_e, 1, n_out) for b in biases])


def _moe_ffn_rows(xg, meta, w_gate, b_gate, w_up, b_up, w_down, b_down, *, tf, tn):
    act = _moe_call(xg, meta, (w_gate, w_up), (b_gate, b_up), tn=tf, swiglu=True, out_dtype=BF16,
                    name="moe_gate_up")
    return _moe_call(act, meta, (w_down,), (b_down,), tn=tn, swiglu=False, out_dtype=F32, name="moe_down")


def _moe_dispatch(top_e, n_experts):
    n_tok = top_e.shape[0]
    n_assign = n_tok * TOP_K
    flat_e = top_e.reshape(-1)
    order = jnp.argsort(flat_e)
    sorted_e = flat_e[order]
    counts = jnp.sum((flat_e[:, None] == jnp.arange(n_experts, dtype=I32)[None, :]).astype(I32), axis=0)
    padded = (counts + MOE_TM - 1) // MOE_TM * MOE_TM
    pad_end = jnp.cumsum(padded)
    pad_start = pad_end - padded
    start = jnp.cumsum(counts) - counts
    dest = pad_start[sorted_e] + jnp.arange(n_assign, dtype=I32) - start[sorted_e]
    nb = (n_assign + n_experts * (MOE_TM - 1) + MOE_TM - 1) // MOE_TM
    row_tok = jnp.zeros((nb * MOE_TM,), I32).at[dest].set((order // TOP_K).astype(I32))
    pos = jnp.zeros((n_assign,), I32).at[order].set(dest)
    n_used = (pad_end[-1] // MOE_TM).astype(I32)
    idx = jnp.arange(nb, dtype=I32)
    blk = jnp.minimum(idx, n_used - 1) * MOE_TM
    blk_e = jnp.minimum(jnp.sum((pad_end[None, :] <= blk[:, None]).astype(I32), axis=1), n_experts - 1)
    prev_e = jnp.concatenate([jnp.full((1,), -1, I32), blk_e[:-1]])
    first = jnp.logical_and(idx < n_used, blk_e != prev_e)
    seg_idx = jnp.cumsum(first.astype(I32)) - 1
    n_seg = jnp.sum(first.astype(I32))
    first_at = jnp.where(first, idx, nb)
    next_first = jnp.concatenate([lax.cummin(first_at, reverse=True)[1:], jnp.full((1,), nb, I32)])
    last_seg = next_first >= nb
    nxt_e = jnp.where(last_seg, blk_e[0], blk_e[jnp.minimum(next_first, nb - 1)])
    meta = (blk_e.astype(I32), n_used.reshape(1), first.astype(I32), nxt_e.astype(I32),
            last_seg.astype(I32), seg_idx.astype(I32), n_seg.reshape(1).astype(I32))
    return row_tok, pos, meta


def _combine_final_kernel(pos_ref, rows_hbm, p_ref, x_ref, gate_ref, w_ref, o_ref, gbuf, sem, *, tl):
    i = pl.program_id(0)
    n = pl.num_programs(0)
    slot = lax.rem(i, 2)

    def row_copy(step, t, k, sl):
        r = pos_ref[(step * tl + t) * TOP_K + k]
        return pltpu.make_async_copy(rows_hbm.at[pl.ds(r, 1), :], gbuf.at[sl, k, pl.ds(t, 1), :], sem.at[sl])

    def for_rows(step, sl, fn):
        def body(t, carry):
            for k in range(TOP_K):
                fn(row_copy(step, t, k, sl))
            return carry

        lax.fori_loop(0, tl, body, 0)

    @pl.when(i == 0)
    def _():
        for_rows(0, 0, lambda cp: cp.start())

    @pl.when(i + 1 < n)
    def _():
        for_rows(i + 1, 1 - slot, lambda cp: cp.start())

    for_rows(i, slot, lambda cp: cp.wait())

    probs = p_ref[...]
    ffn = probs[:, 0:1] * gbuf[slot, 0]
    for k in range(1, TOP_K):
        ffn = ffn + probs[:, k:k + 1] * gbuf[slot, k]
    x = x_ref[...] + gate_ref[...] * ffn
    o_ref[...] = x * lax.rsqrt(jnp.mean(x * x, axis=-1, keepdims=True) + RMS_EPS) * w_ref[...]


def _combine_final(x2d, rows_out, pos, probs, gate, w, seq, *, tl):
    m, d = x2d.shape
    gate_a, gate_s = _mod_operand(gate, seq, tl)
    row = pl.BlockSpec((tl, d), lambda i, pos: (i, 0))
    return pl.pallas_call(
        functools.partial(_combine_final_kernel, tl=tl),
        grid_spec=pltpu.PrefetchScalarGridSpec(
            num_scalar_prefetch=1, grid=(m // tl,),
            in_specs=[pl.BlockSpec(memory_space=pl.ANY),
                      pl.BlockSpec((tl, probs.shape[-1]), lambda i, pos: (i, 0)),
                      row, gate_s(d, lambda i, pos: i, lambda i, pos: 0),
                      pl.BlockSpec((1, d), lambda i, pos: (0, 0))],
            out_specs=row,
            scratch_shapes=[pltpu.VMEM((2, TOP_K, tl, d), F32), pltpu.SemaphoreType.DMA((2,))]),
        out_shape=jax.ShapeDtypeStruct((m, d), F32),
        compiler_params=_cparams(1), name="combine_final",
    )(pos, rows_out, probs, x2d, gate_a, w.reshape(1, d))


def _pad_hist(buf):
    return jnp.pad(buf, ((0, 0), (HIST_ROWS - buf.shape[1], 0), (0, 0)))


def _group_front(x, mod, buf_a, buf_b, ssm_h0, p, *, tl, tr, tm):
    bsz, seq, d = x.shape
    m = bsz * seq
    shift1, scale1, gate1, shift2, scale2, _ = jnp.split(mod, 6, axis=-1)
    conv_ch = p["w_dw"].shape[-1]
    d_ssm = p["ssm_norm_w"].shape[-1]
    n_heads = d_ssm // HEAD_DIM
    hpg = n_heads // SSM_GROUPS
    xbc = p["w_xbc_conv"].shape[-1]
    n_main = 2 * conv_ch + d_ssm + xbc
    x2d = x.reshape(m, d)

    hn1 = _norm_mod(x2d, p["norm1_w"], shift1, scale1, seq, tl=tr)
    proj = _mm(hn1, p["w_in"], jnp.zeros((1, n_main), F32), tm=tm, tn=512, n_out=n_main)
    dt_raw = _mm(hn1, p["w_in"][:, n_main:], jnp.zeros((1, n_heads), F32), tm=tm, tn=n_heads, n_out=n_heads)
    proj3 = proj.reshape(bsz, seq, n_main)

    out_a, hist_a = _conv_a(proj3, p["b_glu"], _pad_hist(buf_a), p["w_dw"], p["b_dw"], p["ln_w"], p["ln_b"], tl=tl)
    new_a = hist_a[:, HIST_ROWS - buf_a.shape[1]:]

    gn = (xbc - d_ssm) // 2
    hists = (_pad_hist(buf_b[..., :d_ssm]), _pad_hist(buf_b[..., d_ssm:d_ssm + gn]), _pad_hist(buf_b[..., d_ssm + gn:]))
    xs_c, bm_c, cm_c, dt, nhx, nhb, nhc = _conv_b(
        proj3, dt_raw.reshape(bsz, seq, n_heads), hists, p["w_xbc_conv"], p["b_xbc_conv"], p["dt_bias"],
        tl=tl, d_ssm=d_ssm, col_xs=2 * conv_ch + d_ssm)
    kb = buf_b.shape[1]
    new_b = jnp.concatenate([nhx, nhb, nhc], axis=-1)[:, HIST_ROWS - kb:]

    q = SSD_CHUNK if seq % SSD_CHUNK == 0 else seq
    dt_g = dt.reshape(bsz, seq, SSM_GROUPS, hpg).transpose(0, 2, 1, 3)
    gpb = SSM_GROUPS if q < SSD_CHUNK else min(2, SSM_GROUPS)
    y, h_fin = _ssd(xs_c, bm_c, cm_c, dt_g, p["a_log"].reshape(SSM_GROUPS, 1, hpg), ssm_h0, q=q, gpb=gpb)
    out_b = _gate_norm(y.reshape(m, d_ssm), xs_c.reshape(m, d_ssm), proj,
                       jnp.repeat(p["d_skip"], HEAD_DIM).reshape(1, d_ssm), p["ssm_norm_w"],
                       tl=tr, col_z=2 * conv_ch, gw=d_ssm // SSM_GROUPS)

    x1 = _out_proj(out_a.reshape(m, conv_ch), out_b, p["w_out"], x2d, gate1, seq, tm=tm, tn=512)
    hn2, top_e, probs = _norm_mod(x1, p["norm2_w"], shift2, scale2, seq, tl=tr,
                                  router=(p["w_router"], p["b_router"]))
    return x1, hn2, top_e[:, :TOP_K], probs, new_a, new_b, h_fin


def kernel(x_prompt, x_sample, state_conv_a, state_conv_ssm, state_ssm, c_prompt, c_sample, norm1_w, norm2_w, final_norm_w, w_ada, b_ada, w_in, b_glu, w_dw, b_dw, ln_w, ln_b, w_xbc_conv, b_xbc_conv, dt_bias, a_log, d_skip, ssm_norm_w, w_out, w_router, b_router, w_gate, b_gate, w_up, b_up, w_down, b_down):
    layer = 0
    p = dict(norm1_w=norm1_w[layer], norm2_w=norm2_w[layer], w_in=w_in[layer], b_glu=b_glu[layer],
             w_dw=w_dw[layer], b_dw=b_dw[layer], ln_w=ln_w[layer], ln_b=ln_b[layer],
             w_xbc_conv=w_xbc_conv[layer], b_xbc_conv=b_xbc_conv[layer], dt_bias=dt_bias[layer],
             a_log=a_log[layer], d_skip=d_skip[layer], ssm_norm_w=ssm_norm_w[layer], w_out=w_out[layer],
             w_router=w_router[layer], b_router=b_router[layer])
    bp, sp, d = x_prompt.shape
    bs, ss, _ = x_sample.shape
    mp, ms = bp * sp, bs * ss
    n_experts = w_router.shape[-1]

    c_all = jnp.concatenate([c_prompt, c_sample], axis=0)
    n_c = c_all.shape[0]
    n_cp = (n_c + 7) // 8 * 8
    c_all = jnp.pad(c_all, ((0, n_cp - n_c), (0, 0)))
    mod = _mm(c_all, w_ada[layer], b_ada[layer].reshape(1, -1), tm=n_cp, tn=512, n_out=w_ada.shape[-1], silu_in=True)

    zeros = lambda a, b: jnp.zeros((b,) + a.shape[2:], a.dtype)
    x1p, hn2p, ep, pp, conv_a_p, conv_b_p, ssm_p = _group_front(
        x_prompt, mod[:bp], zeros(state_conv_a, bp), zeros(state_conv_ssm, bp), zeros(state_ssm, bp), p,
        tl=min(256, sp), tr=min(256, mp), tm=min(512, mp))
    x1s, hn2s, es, ps, conv_a_s, conv_b_s, ssm_s = _group_front(
        x_sample, mod[bp:bp + bs], state_conv_a[layer], state_conv_ssm[layer], state_ssm[layer], p,
        tl=ss, tr=min(256, ms), tm=min(512, ms))

    hn2 = jnp.concatenate([hn2p, hn2s], axis=0)
    top_e = jnp.concatenate([ep, es], axis=0)
    row_tok, pos, meta = _moe_dispatch(top_e, n_experts)
    xg = jnp.take(hn2, row_tok, axis=0, mode="clip")
    rows_out = _moe_ffn_rows(xg, meta, w_gate[layer], b_gate[layer], w_up[layer], b_up[layer],
                             w_down[layer], b_down[layer], tf=512, tn=512)

    gate2_p = mod[:bp, 5 * d:]
    gate2_s = mod[bp:bp + bs, 5 * d:]
    y_p = _combine_final(x1p, rows_out, pos[:mp * TOP_K], pp, gate2_p, final_norm_w, sp, tl=min(128, mp))
    y_s = _combine_final(x1s, rows_out, pos[mp * TOP_K:], ps, gate2_s, final_norm_w, ss, tl=min(128, ms))
    y_p = y_p.reshape(bp, sp, d)
    y_s = y_s.reshape(bs, ss, d)
    return (y_p, y_s, conv_a_p[None], conv_b_p[None], ssm_p[None], conv_a_s[None], conv_b_s[None], ssm_s[None])
```

```python
import functools

import jax
import jax.numpy as jnp
from jax import lax
from jax.experimental import pallas as pl
from jax.experimental.pallas import tpu as pltpu

F32 = jnp.float32
BF16 = jnp.bfloat16
I32 = jnp.int32

HEAD_DIM = 64
SSM_GROUPS = 8
D_STATE = 128
SSD_CHUNK = 128
TOP_K = 4
SWIGLU_LIMIT = 7.0
SWIGLU_ALPHA = 1.702
RMS_EPS = 1e-5
LN_EPS = 1e-5
HIST_ROWS = 32
VMEM_LIMIT = 56 * 1024 * 1024
MOE_TM = 256
MOE_SCRATCH_BYTES = 4 * 1024 * 1024

_HI = lax.Precision.HIGHEST
_NT = (((1,), (1,)), ((), ()))
_TN = (((0,), (0,)), ((), ()))


def _cparams(n_axes, vmem_bytes=VMEM_LIMIT):
    return pltpu.CompilerParams(dimension_semantics=("arbitrary",) * n_axes,
                                vmem_limit_bytes=vmem_bytes)


def _silu(x):
    return x * jax.nn.sigmoid(x)


def _cast_rows(src_ref, dst_ref, chunk):
    rows = src_ref.shape[0]

    def body(c, carry):
        r = pl.multiple_of(c * chunk, chunk)
        dst_ref[pl.ds(r, chunk), :] = src_ref[pl.ds(r, chunk), :].astype(BF16)
        return carry

    lax.fori_loop(0, rows // chunk, body, 0)


def _mm_kernel(x_ref, w_ref, b_ref, o_ref, wb_ref, *, silu_in):
    @pl.when(pl.program_id(1) == 0)
    def _():
        _cast_rows(w_ref, wb_ref, 256)

    x = x_ref[...]
    if silu_in:
        x = _silu(x.astype(F32))
    o_ref[...] = jnp.dot(x.astype(BF16), wb_ref[...], preferred_element_type=F32) + b_ref[...]


def _mm(x, w, b, *, tm, tn, n_out, silu_in=False):
    m, k = x.shape
    assert n_out % tn == 0 and m % tm == 0, (m, tm, n_out, tn)
    return pl.pallas_call(
        functools.partial(_mm_kernel, silu_in=silu_in),
        grid=(n_out // tn, m // tm),
        in_specs=[pl.BlockSpec((tm, k), lambda j, i: (i, 0)),
                  pl.BlockSpec((k, tn), lambda j, i: (0, j)),
                  pl.BlockSpec((1, tn), lambda j, i: (0, j))],
        out_specs=pl.BlockSpec((tm, tn), lambda j, i: (i, j)),
        out_shape=jax.ShapeDtypeStruct((m, n_out), F32),
        scratch_shapes=[pltpu.VMEM((k, tn), BF16)],
        compiler_params=_cparams(2),
        name="dense_mm",
    )(x, w, b)


def _mod_operand(v, seq, tile):
    bsz, d = v.shape
    if seq % tile == 0:
        per = seq // tile
        arr = v.reshape(bsz, 1, d)

        def spec(tn, row_of, col_of):
            return pl.BlockSpec((None, 1, tn), lambda *g: (row_of(*g) // per, 0, col_of(*g)))
    else:
        arr = jnp.repeat(v, seq, axis=0)

        def spec(tn, row_of, col_of):
            return pl.BlockSpec((tile, tn), lambda *g: (row_of(*g), col_of(*g)))
    return arr, spec


def _norm_kernel(x_ref, w_ref, shift_ref, scale_ref, o_ref):
    x = x_ref[...]
    y = x * lax.rsqrt(jnp.mean(x * x, axis=-1, keepdims=True) + RMS_EPS) * w_ref[...]
    o_ref[...] = (y * (1.0 + scale_ref[...]) + shift_ref[...]).astype(o_ref.dtype)


def _norm_router_kernel(x_ref, w_ref, shift_ref, scale_ref, wr_ref, br_ref, o_ref, e_ref, p_ref):
    x = x_ref[...]
    y = x * lax.rsqrt(jnp.mean(x * x, axis=-1, keepdims=True) + RMS_EPS) * w_ref[...]
    hn = y * (1.0 + scale_ref[...]) + shift_ref[...]
    o_ref[...] = hn.astype(o_ref.dtype)
    vals = jnp.dot(hn, wr_ref[...], precision=_HI, preferred_element_type=F32) + br_ref[...]
    n_e = vals.shape[-1]
    lane = lax.broadcasted_iota(I32, vals.shape, 1)
    tops, idxs = [], []
    for _ in range(TOP_K):
        m = jnp.max(vals, axis=-1, keepdims=True)
        idx = jnp.min(jnp.where(vals == m, lane, n_e), axis=-1, keepdims=True)
        tops.append(m)
        idxs.append(idx)
        vals = jnp.where(lane == idx, -jnp.inf, vals)
    exps = [jnp.exp(t - tops[0]) for t in tops]
    denom = exps[0]
    for ex in exps[1:]:
        denom = denom + ex
    out_lane = lax.broadcasted_iota(I32, e_ref.shape, 1)
    e_out = jnp.zeros(e_ref.shape, I32)
    p_out = jnp.zeros(p_ref.shape, F32)
    for k in range(TOP_K):
        e_out = jnp.where(out_lane == k, idxs[k], e_out)
        p_out = jnp.where(out_lane == k, exps[k] / denom, p_out)
    e_ref[...] = e_out
    p_ref[...] = p_out


def _norm_mod(x2d, w, shift, scale, seq, *, tl, router=None):
    m, d = x2d.shape
    shift_a, shift_s = _mod_operand(shift, seq, tl)
    scale_a, scale_s = _mod_operand(scale, seq, tl)
    row_of = lambda i: i
    col_of = lambda i: 0
    in_specs = [pl.BlockSpec((tl, d), lambda i: (i, 0)),
                pl.BlockSpec((1, d), lambda i: (0, 0)),
                shift_s(d, row_of, col_of), scale_s(d, row_of, col_of)]
    args = [x2d, w.reshape(1, d), shift_a, scale_a]
    hn_spec = pl.BlockSpec((tl, d), lambda i: (i, 0))
    hn_shape = jax.ShapeDtypeStruct((m, d), BF16)
    if router is None:
        return pl.pallas_call(
            _norm_kernel, grid=(m // tl,), in_specs=in_specs, out_specs=hn_spec,
            out_shape=hn_shape, compiler_params=_cparams(1), name="ada_norm",
        )(*args)
    w_r, b_r = router
    n_e = w_r.shape[-1]
    in_specs += [pl.BlockSpec((d, n_e), lambda i: (0, 0)), pl.BlockSpec((1, n_e), lambda i: (0, 0))]
    args += [w_r, b_r.reshape(1, n_e)]
    return pl.pallas_call(
        _norm_router_kernel, grid=(m // tl,), in_specs=in_specs,
        out_specs=[hn_spec, pl.BlockSpec((tl, 128), lambda i: (i, 0)), pl.BlockSpec((tl, 128), lambda i: (i, 0))],
        out_shape=[hn_shape, jax.ShapeDtypeStruct((m, 128), I32), jax.ShapeDtypeStruct((m, 128), F32)],
        compiler_params=_cparams(1), name="ada_norm_router",
    )(*args)


def _conv_a_kernel(a_ref, g_ref, bglu_ref, hist_ref, wdw_ref, bdw_ref, lnw_ref, lnb_ref,
                   o_ref, newhist_ref, ubuf, vbuf, *, tl, width, cc):
    ch = a_ref.shape[-1]
    off = HIST_ROWS - (width - 1)

    @pl.when(pl.program_id(1) == 0)
    def _():
        ubuf[0:HIST_ROWS, :] = hist_ref[...]

    a = a_ref[...] + bglu_ref[:, 0:ch]
    g = g_ref[...] + bglu_ref[:, ch:2 * ch]
    ubuf[HIST_ROWS:HIST_ROWS + tl, :] = a * jax.nn.sigmoid(g)

    rc = min(tl, 128)
    for ri in range(tl // rc):
        r0 = ri * rc

        def col_body(ci, carry, r0=r0):
            c0 = pl.multiple_of(ci * cc, cc)
            acc = jnp.broadcast_to(bdw_ref[:, pl.ds(c0, cc)], (rc, cc))
            for k in range(width):
                acc = acc + wdw_ref[k:k + 1, pl.ds(c0, cc)] * ubuf[r0 + off + k:r0 + off + k + rc, pl.ds(c0, cc)]
            vbuf[r0:r0 + rc, pl.ds(c0, cc)] = acc
            return carry

        lax.fori_loop(0, ch // cc, col_body, 0)

        v = vbuf[r0:r0 + rc, :]
        mu = jnp.mean(v, axis=-1, keepdims=True)
        vc = v - mu
        var = jnp.mean(vc * vc, axis=-1, keepdims=True)
        y = vc * lax.rsqrt(var + LN_EPS) * lnw_ref[...] + lnb_ref[...]
        o_ref[r0:r0 + rc, :] = _silu(y).astype(o_ref.dtype)

    tail = ubuf[tl:tl + HIST_ROWS, :]
    newhist_ref[...] = tail
    ubuf[0:HIST_ROWS, :] = tail


def _conv_a(proj3, b_glu, hist, w_dw, b_dw, ln_w, ln_b, *, tl):
    bsz, seq, _ = proj3.shape
    width, ch = w_dw.shape
    cc = 256
    kern = functools.partial(_conv_a_kernel, tl=tl, width=width, cc=cc)
    full = lambda shape: pl.BlockSpec(shape, lambda b, t: (0,) * len(shape))
    return pl.pallas_call(
        kern, grid=(bsz, seq // tl),
        in_specs=[pl.BlockSpec((None, tl, ch), lambda b, t: (b, t, 0)),
                  pl.BlockSpec((None, tl, ch), lambda b, t: (b, t, 1)),
                  full((1, 2 * ch)),
                  pl.BlockSpec((None, HIST_ROWS, ch), lambda b, t: (b, 0, 0)),
                  full((width, ch)), full((1, ch)), full((1, ch)), full((1, ch))],
        out_specs=[pl.BlockSpec((None, tl, ch), lambda b, t: (b, t, 0)),
                   pl.BlockSpec((None, HIST_ROWS, ch), lambda b, t: (b, 0, 0))],
        out_shape=[jax.ShapeDtypeStruct((bsz, seq, ch), BF16),
                   jax.ShapeDtypeStruct((bsz, HIST_ROWS, ch), F32)],
        scratch_shapes=[pltpu.VMEM((HIST_ROWS + tl, ch), F32), pltpu.VMEM((tl, ch), F32)],
        compiler_params=_cparams(2), name="conformer_conv",
    )(proj3, proj3, b_glu.reshape(1, -1), hist, w_dw, b_dw.reshape(1, -1),
      ln_w.reshape(1, -1), ln_b.reshape(1, -1))


def _conv_b_kernel(xs_ref, bm_ref, cm_ref, dtr_ref, hx_ref, hb_ref, hc_ref, w_ref, b_ref, dtb_ref,
                   oxs_ref, obm_ref, ocm_ref, odt_ref, nhx_ref, nhb_ref, nhc_ref,
                   bx, bb, bc, *, tl, width, cc):
    off = HIST_ROWS - (width - 1)
    segs = ((xs_ref, hx_ref, oxs_ref, nhx_ref, bx), (bm_ref, hb_ref, obm_ref, nhb_ref, bb),
            (cm_ref, hc_ref, ocm_ref, nhc_ref, bc))
    first = pl.program_id(1) == 0
    rc = min(tl, 128)
    col0 = 0
    for in_ref, h_ref, out_ref, nh_ref, buf in segs:
        wseg = in_ref.shape[-1]

        @pl.when(first)
        def _(buf=buf, h_ref=h_ref):
            buf[0:HIST_ROWS, :] = h_ref[...]

        buf[HIST_ROWS:HIST_ROWS + tl, :] = in_ref[...]
        for ri in range(tl // rc):
            r0 = ri * rc

            def col_body(ci, carry, r0=r0, buf=buf, out_ref=out_ref, col0=col0):
                c0 = pl.multiple_of(ci * cc, cc)
                acc = jnp.broadcast_to(b_ref[:, pl.ds(col0 + c0, cc)], (rc, cc))
                for k in range(width):
                    acc = acc + w_ref[k:k + 1, pl.ds(col0 + c0, cc)] * buf[r0 + off + k:r0 + off + k + rc, pl.ds(c0, cc)]
                out_ref[r0:r0 + rc, pl.ds(c0, cc)] = _silu(acc)
                return carry

            lax.fori_loop(0, wseg // cc, col_body, 0)
        tail = buf[tl:tl + HIST_ROWS, :]
        nh_ref[...] = tail
        buf[0:HIST_ROWS, :] = tail
        col0 += wseg

    t = dtr_ref[...] + dtb_ref[...]
    odt_ref[...] = jnp.maximum(t, 0.0) + jnp.log1p(jnp.exp(-jnp.abs(t)))


def _conv_b(proj3, dt_raw3, hists, w_conv, b_conv, dt_bias, *, tl, d_ssm, col_xs):
    bsz, seq, _ = proj3.shape
    width, xbc = w_conv.shape
    gn = (xbc - d_ssm) // 2
    nh = dt_raw3.shape[-1]
    cc = 128
    kern = functools.partial(_conv_b_kernel, tl=tl, width=width, cc=cc)
    full = lambda shape: pl.BlockSpec(shape, lambda b, t: (0,) * len(shape))
    ix, ib, ic = col_xs // d_ssm, (col_xs + d_ssm) // gn, (col_xs + d_ssm + gn) // gn
    hspec = lambda w: pl.BlockSpec((None, HIST_ROWS, w), lambda b, t: (b, 0, 0))
    ospec = lambda w: pl.BlockSpec((None, tl, w), lambda b, t: (b, t, 0))
    return pl.pallas_call(
        kern, grid=(bsz, seq // tl),
        in_specs=[pl.BlockSpec((None, tl, d_ssm), lambda b, t: (b, t, ix)),
                  pl.BlockSpec((None, tl, gn), lambda b, t: (b, t, ib)),
                  pl.BlockSpec((None, tl, gn), lambda b, t: (b, t, ic)),
                  ospec(nh), hspec(d_ssm), hspec(gn), hspec(gn),
                  full((width, xbc)), full((1, xbc)), full((1, nh))],
        out_specs=[ospec(d_ssm), ospec(gn), ospec(gn), ospec(nh), hspec(d_ssm), hspec(gn), hspec(gn)],
        out_shape=[jax.ShapeDtypeStruct((bsz, seq, d_ssm), F32),
                   jax.ShapeDtypeStruct((bsz, seq, gn), F32),
                   jax.ShapeDtypeStruct((bsz, seq, gn), F32),
                   jax.ShapeDtypeStruct((bsz, seq, nh), F32),
                   jax.ShapeDtypeStruct((bsz, HIST_ROWS, d_ssm), F32),
                   jax.ShapeDtypeStruct((bsz, HIST_ROWS, gn), F32),
                   jax.ShapeDtypeStruct((bsz, HIST_ROWS, gn), F32)],
        scratch_shapes=[pltpu.VMEM((HIST_ROWS + tl, d_ssm), F32),
                        pltpu.VMEM((HIST_ROWS + tl, gn), F32),
                        pltpu.VMEM((HIST_ROWS + tl, gn), F32)],
        compiler_params=_cparams(2), name="ssm_conv",
    )(proj3, proj3, proj3, dt_raw3, *hists, w_conv, b_conv.reshape(1, -1), dt_bias.reshape(1, -1))


def _ssd_kernel(xs_ref, bm_ref, cm_ref, dt_ref, alog_ref, h0_ref, y_ref, hout_ref, h_scr, xdd_scr, *, q, hpg, gpb):
    c = pl.program_id(2)

    @pl.when(c == 0)
    def _():
        h_scr[...] = h0_ref[...]

    p = HEAD_DIM
    n = D_STATE
    gw = hpg * p
    row = lax.broadcasted_iota(I32, (q, q), 0)
    col = lax.broadcasted_iota(I32, (q, q), 1)
    causal = row >= col
    tril = causal.astype(F32)
    for gi in range(gpb):
        bmb = bm_ref[:, gi * n:(gi + 1) * n].astype(BF16)
        cmb = cm_ref[:, gi * n:(gi + 1) * n].astype(BF16)
        dt = dt_ref[gi]
        da = dt * (-jnp.exp(alog_ref[gi]))
        cum = jnp.dot(tril, da, precision=_HI, preferred_element_type=F32)
        cum_t = cum.T
        dt_t = dt.T
        cb = lax.dot_general(cmb, bmb, _NT, preferred_element_type=F32)
        cum_last = cum[q - 1:q, :]
        dd = jnp.exp(cum_last - cum) * dt
        ecum = jnp.exp(cum)
        cdec = jnp.exp(cum_last)
        hin = h_scr[gi]
        yo = lax.dot_general(cmb, hin.astype(BF16), _NT, preferred_element_type=F32)
        for r in range(hpg):
            c0 = gi * gw + r * p
            xr = xs_ref[:, c0:c0 + p]
            diff = cum[:, r:r + 1] - cum_t[r:r + 1, :]
            decay = jnp.exp(jnp.where(causal, diff, -jnp.inf))
            w = cb * decay * dt_t[r:r + 1, :]
            yd = jnp.dot(w.astype(BF16), xr.astype(BF16), preferred_element_type=F32)
            y_ref[:, c0:c0 + p] = yd + yo[:, r * p:(r + 1) * p] * ecum[:, r:r + 1]
            xdd_scr[:, c0:c0 + p] = xr * dd[:, r:r + 1]
        xdd = xdd_scr[:, gi * gw:(gi + 1) * gw].astype(BF16)
        st = lax.dot_general(xdd, bmb, _TN, preferred_element_type=F32)
        for r in range(hpg):
            rows = slice(r * p, (r + 1) * p)
            h_scr[gi, rows, :] = cdec[:, r:r + 1] * hin[rows, :] + st[rows, :]

    @pl.when(c == pl.num_programs(2) - 1)
    def _():
        hout_ref[...] = h_scr[...]


def _ssd(xs_c, bm_c, cm_c, dt_g, a_log_g, h0, *, q, gpb):
    bsz, seq, d_ssm = xs_c.shape
    n_groups = dt_g.shape[1]
    hpg = dt_g.shape[-1]
    gw = hpg * HEAD_DIM
    n = D_STATE
    kern = functools.partial(_ssd_kernel, q=q, hpg=hpg, gpb=gpb)
    y, h_fin = pl.pallas_call(
        kern, grid=(bsz, n_groups // gpb, seq // q),
        in_specs=[pl.BlockSpec((None, q, gpb * gw), lambda b, g, c: (b, c, g)),
                  pl.BlockSpec((None, q, gpb * n), lambda b, g, c: (b, c, g)),
                  pl.BlockSpec((None, q, gpb * n), lambda b, g, c: (b, c, g)),
                  pl.BlockSpec((None, gpb, q, hpg), lambda b, g, c: (b, g, c, 0)),
                  pl.BlockSpec((gpb, 1, hpg), lambda b, g, c: (g, 0, 0)),
                  pl.BlockSpec((None, gpb, gw, n), lambda b, g, c: (b, g, 0, 0))],
        out_specs=[pl.BlockSpec((None, q, gpb * gw), lambda b, g, c: (b, c, g)),
                   pl.BlockSpec((None, gpb, gw, n), lambda b, g, c: (b, g, 0, 0))],
        out_shape=[jax.ShapeDtypeStruct((bsz, seq, d_ssm), F32),
                   jax.ShapeDtypeStruct((bsz, n_groups, gw, n), F32)],
        scratch_shapes=[pltpu.VMEM((gpb, gw, n), F32), pltpu.VMEM((q, gpb * gw), F32)],
        compiler_params=_cparams(3), name="ssd_scan",
    )(xs_c, bm_c, cm_c, dt_g, a_log_g, h0.reshape(bsz, n_groups, gw, n))
    return y, h_fin.reshape(h0.shape)


def _gate_norm_kernel(y_ref, xs_ref, z_ref, dsk_ref, nw_ref, o_ref, *, gw):
    y = y_ref[...] + xs_ref[...] * dsk_ref[...]
    yz = y * _silu(z_ref[...])
    for g in range(y.shape[-1] // gw):
        seg = yz[:, g * gw:(g + 1) * gw]
        seg = seg * lax.rsqrt(jnp.mean(seg * seg, axis=-1, keepdims=True) + RMS_EPS)
        o_ref[:, g * gw:(g + 1) * gw] = (seg * nw_ref[:, g * gw:(g + 1) * gw]).astype(o_ref.dtype)


def _gate_norm(y2, xs2, proj2, d_skip_lanes, norm_w, *, tl, col_z, gw):
    m, d_ssm = y2.shape
    row = pl.BlockSpec((tl, d_ssm), lambda i: (i, 0))
    return pl.pallas_call(
        functools.partial(_gate_norm_kernel, gw=gw), grid=(m // tl,),
        in_specs=[row, row, pl.BlockSpec((tl, d_ssm), lambda i: (i, col_z // d_ssm)),
                  pl.BlockSpec((1, d_ssm), lambda i: (0, 0)), pl.BlockSpec((1, d_ssm), lambda i: (0, 0))],
        out_specs=row, out_shape=jax.ShapeDtypeStruct((m, d_ssm), BF16),
        compiler_params=_cparams(1), name="ssm_gate_norm",
    )(y2, xs2, proj2, d_skip_lanes, norm_w.reshape(1, -1))


def _out_proj_kernel(xa_ref, xb_ref, w_ref, x_ref, gate_ref, o_ref, wb_ref):
    @pl.when(pl.program_id(1) == 0)
    def _():
        _cast_rows(w_ref, wb_ref, 256)

    ka = xa_ref.shape[-1]
    kb = xb_ref.shape[-1]
    mixed = jnp.dot(xa_ref[...], wb_ref[0:ka, :], preferred_element_type=F32)
    mixed = mixed + jnp.dot(xb_ref[...], wb_ref[ka:ka + kb, :], preferred_element_type=F32)
    o_ref[...] = x_ref[...] + gate_ref[...] * mixed


def _out_proj(xa, xb, w_out, x2d, gate, seq, *, tm, tn):
    m, d = x2d.shape
    ka, kb = xa.shape[-1], xb.shape[-1]
    assert d % tn == 0 and m % tm == 0, (m, tm, d, tn)
    gate_a, gate_s = _mod_operand(gate, seq, tm)
    return pl.pallas_call(
        _out_proj_kernel, grid=(d // tn, m // tm),
        in_specs=[pl.BlockSpec((tm, ka), lambda j, i: (i, 0)),
                  pl.BlockSpec((tm, kb), lambda j, i: (i, 0)),
                  pl.BlockSpec((ka + kb, tn), lambda j, i: (0, j)),
                  pl.BlockSpec((tm, tn), lambda j, i: (i, j)),
                  gate_s(tn, lambda j, i: i, lambda j, i: j)],
        out_specs=pl.BlockSpec((tm, tn), lambda j, i: (i, j)),
        out_shape=jax.ShapeDtypeStruct((m, d), F32),
        scratch_shapes=[pltpu.VMEM((ka + kb, tn), BF16)],
        compiler_params=_cparams(2), name="out_proj",
    )(xa, xb, w_out, x2d, gate_a)


N_SEG_META = 5


def _moe_kernel(be_ref, nu_ref, first_ref, nxt_ref, lastseg_ref, x_ref, *rest, n_w, swiglu):
    w_hbm = rest[:n_w]
    b_refs = rest[n_w:2 * n_w]
    o_ref = rest[2 * n_w]
    wbuf, wb, sem = rest[2 * n_w + 1:]
    j = pl.program_id(0)
    i = pl.program_id(1)
    n_j = pl.num_programs(0)
    tf = wb.shape[-1]

    def w_copy(t, e, jj):
        cols = pl.ds(pl.multiple_of(jj * tf, tf), tf)
        return pltpu.make_async_copy(w_hbm[t].at[e, :, cols], wbuf.at[t], sem.at[t])

    @pl.when(i < nu_ref[0])
    def _():
        @pl.when(first_ref[i] == 1)
        def _():
            e = be_ref[i]

            @pl.when(jnp.logical_and(i == 0, j == 0))
            def _():
                for t in range(n_w):
                    w_copy(t, e, j).start()

            last = lastseg_ref[i]
            has_next = jnp.logical_not(jnp.logical_and(last == 1, j == n_j - 1))
            for t in range(n_w):
                w_copy(t, e, j).wait()
                _cast_rows(wbuf.at[t], wb.at[t], 256)

                @pl.when(has_next)
                def _(t=t):
                    w_copy(t, nxt_ref[i], j + last).start()

        x = x_ref[...]
        if swiglu:
            g = jnp.dot(x, wb[0], preferred_element_type=F32) + b_refs[0][...]
            u = jnp.dot(x, wb[1], preferred_element_type=F32) + b_refs[1][...]
            g = jnp.minimum(g, SWIGLU_LIMIT)
            u = jnp.clip(u, -SWIGLU_LIMIT, SWIGLU_LIMIT)
            o_ref[...] = ((u + 1.0) * (g * jax.nn.sigmoid(SWIGLU_ALPHA * g))).astype(o_ref.dtype)
        else:
            o_ref[...] = jnp.dot(x, wb[0], preferred_element_type=F32) + b_refs[0][...]

    @pl.when(i >= nu_ref[0])
    def _():
        o_ref[...] = jnp.zeros(o_ref.shape, o_ref.dtype)


def _moe_call(x, meta, weights, biases, *, tn, swiglu, out_dtype, name):
    rows, k = x.shape
    n_e, _, n_out = weights[0].shape
    n_w = len(weights)
    assert n_out % tn == 0 and rows % MOE_TM == 0, (rows, n_out, tn)
    nb = rows // MOE_TM
    vmem = (n_w * k * tn * 6 + 2 * MOE_TM * k * 2 + 2 * MOE_TM * tn * jnp.dtype(out_dtype).itemsize
            + MOE_SCRATCH_BYTES)
    row_of = lambda j, i, be, nu, *_: jnp.minimum(i, nu[0] - 1)
    bias_spec = pl.BlockSpec((None, 1, tn), lambda j, i, be, *_: (be[i], 0, j))
    return pl.pallas_call(
        functools.partial(_moe_kernel, n_w=n_w, swiglu=swiglu),
        grid_spec=pltpu.PrefetchScalarGridSpec(
            num_scalar_prefetch=N_SEG_META, grid=(n_out // tn, nb),
            in_specs=[pl.BlockSpec((MOE_TM, k), lambda j, i, *s: (row_of(j, i, *s), 0))]
            + [pl.BlockSpec(memory_space=pl.ANY)] * n_w + [bias_spec] * n_w,
            out_specs=pl.BlockSpec((MOE_TM, tn), lambda j, i, *s: (i, j)),
            scratch_shapes=[pltpu.VMEM((n_w, k, tn), F32), pltpu.VMEM((n_w, k, tn), BF16),
                            pltpu.SemaphoreType.DMA((n_w,))]),
        out_shape=jax.ShapeDtypeStruct((rows, n_out), out_dtype),
        compiler_params=_cparams(2, vmem), name=name,
    )(*meta, x, *weights, *[b.reshape(n_e, 1, n_out) for b in biases])


def _moe_ffn_rows(xg, meta, w_gate, b_gate, w_up, b_up, w_down, b_down, *, tf, tn):
    act = _moe_call(xg, meta, (w_gate, w_up), (b_gate, b_up), tn=tf, swiglu=True, out_dtype=BF16,
                    name="moe_gate_up")
    return _moe_call(act, meta, (w_down,), (b_down,), tn=tn, swiglu=False, out_dtype=F32, name="moe_down")


def _moe_dispatch(top_e, n_experts):
    n_tok = top_e.shape[0]
    n_assign = n_tok * TOP_K
    flat_e = top_e.reshape(-1)
    order = jnp.argsort(flat_e)
    sorted_e = flat_e[order]
    counts = jnp.sum((flat_e[:, None] == jnp.arange(n_experts, dtype=I32)[None, :]).astype(I32), axis=0)
    padded = (counts + MOE_TM - 1) // MOE_TM * MOE_TM
    pad_end = jnp.cumsum(padded)
    pad_start = pad_end - padded
    start = jnp.cumsum(counts) - counts
    dest = pad_start[sorted_e] + jnp.arange(n_assign, dtype=I32) - start[sorted_e]
    nb = (n_assign + n_experts * (MOE_TM - 1) + MOE_TM - 1) // MOE_TM
    row_tok = jnp.zeros((nb * MOE_TM,), I32).at[dest].set((order // TOP_K).astype(I32))
    pos = jnp.zeros((n_assign,), I32).at[order].set(dest)
    n_used = (pad_end[-1] // MOE_TM).astype(I32)
    idx = jnp.arange(nb, dtype=I32)
    blk = jnp.minimum(idx, n_used - 1) * MOE_TM
    blk_e = jnp.minimum(jnp.sum((pad_end[None, :] <= blk[:, None]).astype(I32), axis=1), n_experts - 1)
    prev_e = jnp.concatenate([jnp.full((1,), -1, I32), blk_e[:-1]])
    first = jnp.logical_and(idx < n_used, blk_e != prev_e)
    first_at = jnp.where(first, idx, nb)
    next_first = jnp.concatenate([lax.cummin(first_at, reverse=True)[1:], jnp.full((1,), nb, I32)])
    last_seg = next_first >= nb
    nxt_e = jnp.where(last_seg, blk_e[0], blk_e[jnp.minimum(next_first, nb - 1)])
    meta = (blk_e.astype(I32), n_used.reshape(1), first.astype(I32), nxt_e.astype(I32), last_seg.astype(I32))
    return row_tok, pos, meta


def _combine_final_kernel(pos_ref, rows_hbm, p_ref, x_ref, gate_ref, w_ref, o_ref, gbuf, sem, *, tl):
    i = pl.program_id(0)
    n = pl.num_programs(0)
    slot = lax.rem(i, 2)

    def row_copy(step, t, k, sl):
        r = pos_ref[(step * tl + t) * TOP_K + k]
        return pltpu.make_async_copy(rows_hbm.at[pl.ds(r, 1), :], gbuf.at[sl, k, pl.ds(t, 1), :], sem.at[sl])

    def for_rows(step, sl, fn):
        def body(t, carry):
            for k in range(TOP_K):
                fn(row_copy(step, t, k, sl))
            return carry

        lax.fori_loop(0, tl, body, 0)

    @pl.when(i == 0)
    def _():
        for_rows(0, 0, lambda cp: cp.start())

    @pl.when(i + 1 < n)
    def _():
        for_rows(i + 1, 1 - slot, lambda cp: cp.start())

    for_rows(i, slot, lambda cp: cp.wait())

    probs = p_ref[...]
    ffn = probs[:, 0:1] * gbuf[slot, 0]
    for k in range(1, TOP_K):
        ffn = ffn + probs[:, k:k + 1] * gbuf[slot, k]
    x = x_ref[...] + gate_ref[...] * ffn
    o_ref[...] = x * lax.rsqrt(jnp.mean(x * x, axis=-1, keepdims=True) + RMS_EPS) * w_ref[...]


def _combine_final(x2d, rows_out, pos, probs, gate, w, seq, *, tl):
    m, d = x2d.shape
    gate_a, gate_s = _mod_operand(gate, seq, tl)
    row = pl.BlockSpec((tl, d), lambda i, pos: (i, 0))
    return pl.pallas_call(
        functools.partial(_combine_final_kernel, tl=tl),
        grid_spec=pltpu.PrefetchScalarGridSpec(
            num_scalar_prefetch=1, grid=(m // tl,),
            in_specs=[pl.BlockSpec(memory_space=pl.ANY),
                      pl.BlockSpec((tl, probs.shape[-1]), lambda i, pos: (i, 0)),
                      row, gate_s(d, lambda i, pos: i, lambda i, pos: 0),
                      pl.BlockSpec((1, d), lambda i, pos: (0, 0))],
            out_specs=row,
            scratch_shapes=[pltpu.VMEM((2, TOP_K, tl, d), F32), pltpu.SemaphoreType.DMA((2,))]),
        out_shape=jax.ShapeDtypeStruct((m, d), F32),
        compiler_params=_cparams(1), name="combine_final",
    )(pos, rows_out, probs, x2d, gate_a, w.reshape(1, d))


def _pad_hist(buf):
    return jnp.pad(buf, ((0, 0), (HIST_ROWS - buf.shape[1], 0), (0, 0)))


def _group_front(x, mod, buf_a, buf_b, ssm_h0, p, *, tl, tr, tm):
    bsz, seq, d = x.shape
    m = bsz * seq
    shift1, scale1, gate1, shift2, scale2, _ = jnp.split(mod, 6, axis=-1)
    conv_ch = p["w_dw"].shape[-1]
    d_ssm = p["ssm_norm_w"].shape[-1]
    n_heads = d_ssm // HEAD_DIM
    hpg = n_heads // SSM_GROUPS
    xbc = p["w_xbc_conv"].shape[-1]
    n_main = 2 * conv_ch + d_ssm + xbc
    x2d = x.reshape(m, d)

    hn1 = _norm_mod(x2d, p["norm1_w"], shift1, scale1, seq, tl=tr)
    proj = _mm(hn1, p["w_in"], jnp.zeros((1, n_main), F32), tm=tm, tn=1024 if n_main % 1024 == 0 else 512,
               n_out=n_main)
    dt_raw = _mm(hn1, p["w_in"][:, n_main:], jnp.zeros((1, n_heads), F32), tm=tm, tn=n_heads, n_out=n_heads)
    proj3 = proj.reshape(bsz, seq, n_main)

    out_a, hist_a = _conv_a(proj3, p["b_glu"], _pad_hist(buf_a), p["w_dw"], p["b_dw"], p["ln_w"], p["ln_b"], tl=tl)
    new_a = hist_a[:, HIST_ROWS - buf_a.shape[1]:]

    gn = (xbc - d_ssm) // 2
    hists = (_pad_hist(buf_b[..., :d_ssm]), _pad_hist(buf_b[..., d_ssm:d_ssm + gn]), _pad_hist(buf_b[..., d_ssm + gn:]))
    xs_c, bm_c, cm_c, dt, nhx, nhb, nhc = _conv_b(
        proj3, dt_raw.reshape(bsz, seq, n_heads), hists, p["w_xbc_conv"], p["b_xbc_conv"], p["dt_bias"],
        tl=tl, d_ssm=d_ssm, col_xs=2 * conv_ch + d_ssm)
    kb = buf_b.shape[1]
    new_b = jnp.concatenate([nhx, nhb, nhc], axis=-1)[:, HIST_ROWS - kb:]

    q = SSD_CHUNK if seq % SSD_CHUNK == 0 else seq
    dt_g = dt.reshape(bsz, seq, SSM_GROUPS, hpg).transpose(0, 2, 1, 3)
    gpb = SSM_GROUPS if q < SSD_CHUNK else min(2, SSM_GROUPS)
    y, h_fin = _ssd(xs_c, bm_c, cm_c, dt_g, p["a_log"].reshape(SSM_GROUPS, 1, hpg), ssm_h0, q=q, gpb=gpb)
    out_b = _gate_norm(y.reshape(m, d_ssm), xs_c.reshape(m, d_ssm), proj,
                       jnp.repeat(p["d_skip"], HEAD_DIM).reshape(1, d_ssm), p["ssm_norm_w"],
                       tl=tr, col_z=2 * conv_ch, gw=d_ssm // SSM_GROUPS)

    x1 = _out_proj(out_a.reshape(m, conv_ch), out_b, p["w_out"], x2d, gate1, seq, tm=tm, tn=512)
    hn2, top_e, probs = _norm_mod(x1, p["norm2_w"], shift2, scale2, seq, tl=tr,
                                  router=(p["w_router"], p["b_router"]))
    return x1, hn2, top_e[:, :TOP_K], probs, new_a, new_b, h_fin


def kernel(x_prompt, x_sample, state_conv_a, state_conv_ssm, state_ssm, c_prompt, c_sample, norm1_w, norm2_w, final_norm_w, w_ada, b_ada, w_in, b_glu, w_dw, b_dw, ln_w, ln_b, w_xbc_conv, b_xbc_conv, dt_bias, a_log, d_skip, ssm_norm_w, w_out, w_router, b_router, w_gate, b_gate, w_up, b_up, w_down, b_down):
    layer = 0
    p = dict(norm1_w=norm1_w[layer], norm2_w=norm2_w[layer], w_in=w_in[layer], b_glu=b_glu[layer],
             w_dw=w_dw[layer], b_dw=b_dw[layer], ln_w=ln_w[layer], ln_b=ln_b[layer],
             w_xbc_conv=w_xbc_conv[layer], b_xbc_conv=b_xbc_conv[layer], dt_bias=dt_bias[layer],
             a_log=a_log[layer], d_skip=d_skip[layer], ssm_norm_w=ssm_norm_w[layer], w_out=w_out[layer],
             w_router=w_router[layer], b_router=b_router[layer])
    bp, sp, d = x_prompt.shape
    bs, ss, _ = x_sample.shape
    mp, ms = bp * sp, bs * ss
    n_experts = w_router.shape[-1]

    c_all = jnp.concatenate([c_prompt, c_sample], axis=0)
    n_c = c_all.shape[0]
    n_cp = (n_c + 7) // 8 * 8
    c_all = jnp.pad(c_all, ((0, n_cp - n_c), (0, 0)))
    mod = _mm(c_all, w_ada[layer], b_ada[layer].reshape(1, -1), tm=n_cp, tn=512, n_out=w_ada.shape[-1], silu_in=True)

    zeros = lambda a, b: jnp.zeros((b,) + a.shape[2:], a.dtype)
    x1p, hn2p, ep, pp, conv_a_p, conv_b_p, ssm_p = _group_front(
        x_prompt, mod[:bp], zeros(state_conv_a, bp), zeros(state_conv_ssm, bp), zeros(state_ssm, bp), p,
        tl=min(256, sp), tr=min(256, mp), tm=min(512, mp))
    x1s, hn2s, es, ps, conv_a_s, conv_b_s, ssm_s = _group_front(
        x_sample, mod[bp:bp + bs], state_conv_a[layer], state_conv_ssm[layer], state_ssm[layer], p,
        tl=ss, tr=min(256, ms), tm=min(512, ms))

    hn2 = jnp.concatenate([hn2p, hn2s], axis=0)
    top_e = jnp.concatenate([ep, es], axis=0)
    row_tok, pos, meta = _moe_dispatch(top_e, n_experts)
    xg = hn2.at[row_tok].get(mode="promise_in_bounds")
    rows_out = _moe_ffn_rows(xg, meta, w_gate[layer], b_gate[layer], w_up[layer], b_up[layer],
                             w_down[layer], b_down[layer], tf=min(1024, w_gate.shape[-1]), tn=min(1024, d))

    gate2_p = mod[:bp, 5 * d:]
    gate2_s = mod[bp:bp + bs, 5 * d:]
    y_p = _combine_final(x1p, rows_out, pos[:mp * TOP_K], pp, gate2_p, final_norm_w, sp, tl=min(128, mp))
    y_s = _combine_final(x1s, rows_out, pos[mp * TOP_K:], ps, gate2_s, final_norm_w, ss, tl=min(128, ms))
    y_p = y_p.reshape(bp, sp, d)
    y_s = y_s.reshape(bs, ss, d)
    return (y_p, y_s, conv_a_p[None], conv_b_p[None], ssm_p[None], conv_a_s[None], conv_b_s[None], ssm_s[None])
```

```python
import functools

import jax
import jax.numpy as jnp
from jax import lax
from jax.experimental import pallas as pl
from jax.experimental.pallas import tpu as pltpu

F32 = jnp.float32
BF16 = jnp.bfloat16
I32 = jnp.int32

HEAD_DIM = 64
SSM_GROUPS = 8
D_STATE = 128
SSD_CHUNK = 128
TOP_K = 4
SWIGLU_LIMIT = 7.0
SWIGLU_ALPHA = 1.702
RMS_EPS = 1e-5
LN_EPS = 1e-5
HIST_ROWS = 32
VMEM_LIMIT = 56 * 1024 * 1024
MOE_TM = 256
MOE_SCRATCH_BYTES = 4 * 1024 * 1024

_HI = lax.Precision.HIGHEST
_NT = (((1,), (1,)), ((), ()))
_TN = (((0,), (0,)), ((), ()))


def _cparams(n_axes, vmem_bytes=VMEM_LIMIT):
    return pltpu.CompilerParams(dimension_semantics=("arbitrary",) * n_axes,
                                vmem_limit_bytes=vmem_bytes)


def _silu(x):
    return x * jax.nn.sigmoid(x)


def _cast_rows(src_ref, dst_ref, chunk):
    rows = src_ref.shape[0]

    def body(c, carry):
        r = pl.multiple_of(c * chunk, chunk)
        dst_ref[pl.ds(r, chunk), :] = src_ref[pl.ds(r, chunk), :].astype(BF16)
        return carry

    lax.fori_loop(0, rows // chunk, body, 0)


def _mm_kernel(x_ref, w_ref, b_ref, o_ref, wb_ref, *, silu_in):
    @pl.when(pl.program_id(1) == 0)
    def _():
        _cast_rows(w_ref, wb_ref, 256)

    x = x_ref[...]
    if silu_in:
        x = _silu(x.astype(F32))
    o_ref[...] = jnp.dot(x.astype(BF16), wb_ref[...], preferred_element_type=F32) + b_ref[...]


def _mm(x, w, b, *, tm, tn, n_out, silu_in=False):
    m, k = x.shape
    assert n_out % tn == 0 and m % tm == 0, (m, tm, n_out, tn)
    return pl.pallas_call(
        functools.partial(_mm_kernel, silu_in=silu_in),
        grid=(n_out // tn, m // tm),
        in_specs=[pl.BlockSpec((tm, k), lambda j, i: (i, 0)),
                  pl.BlockSpec((k, tn), lambda j, i: (0, j)),
                  pl.BlockSpec((1, tn), lambda j, i: (0, j))],
        out_specs=pl.BlockSpec((tm, tn), lambda j, i: (i, j)),
        out_shape=jax.ShapeDtypeStruct((m, n_out), F32),
        scratch_shapes=[pltpu.VMEM((k, tn), BF16)],
        compiler_params=_cparams(2),
        name="dense_mm",
    )(x, w, b)


def _mod_operand(v, seq, tile):
    bsz, d = v.shape
    if seq % tile == 0:
        per = seq // tile
        arr = v.reshape(bsz, 1, d)

        def spec(tn, row_of, col_of):
            return pl.BlockSpec((None, 1, tn), lambda *g: (row_of(*g) // per, 0, col_of(*g)))
    else:
        arr = jnp.repeat(v, seq, axis=0)

        def spec(tn, row_of, col_of):
            return pl.BlockSpec((tile, tn), lambda *g: (row_of(*g), col_of(*g)))
    return arr, spec


def _norm_kernel(x_ref, w_ref, shift_ref, scale_ref, o_ref):
    x = x_ref[...]
    y = x * lax.rsqrt(jnp.mean(x * x, axis=-1, keepdims=True) + RMS_EPS) * w_ref[...]
    o_ref[...] = (y * (1.0 + scale_ref[...]) + shift_ref[...]).astype(o_ref.dtype)


def _norm_router_kernel(x_ref, w_ref, shift_ref, scale_ref, wr_ref, br_ref, o_ref, e_ref, p_ref):
    x = x_ref[...]
    y = x * lax.rsqrt(jnp.mean(x * x, axis=-1, keepdims=True) + RMS_EPS) * w_ref[...]
    hn = y * (1.0 + scale_ref[...]) + shift_ref[...]
    o_ref[...] = hn.astype(o_ref.dtype)
    vals = jnp.dot(hn, wr_ref[...], precision=_HI, preferred_element_type=F32) + br_ref[...]
    n_e = vals.shape[-1]
    lane = lax.broadcasted_iota(I32, vals.shape, 1)
    tops, idxs = [], []
    for _ in range(TOP_K):
        m = jnp.max(vals, axis=-1, keepdims=True)
        idx = jnp.min(jnp.where(vals == m, lane, n_e), axis=-1, keepdims=True)
        tops.append(m)
        idxs.append(idx)
        vals = jnp.where(lane == idx, -jnp.inf, vals)
    exps = [jnp.exp(t - tops[0]) for t in tops]
    denom = exps[0]
    for ex in exps[1:]:
        denom = denom + ex
    out_lane = lax.broadcasted_iota(I32, e_ref.shape, 1)
    e_out = jnp.zeros(e_ref.shape, I32)
    p_out = jnp.zeros(p_ref.shape, F32)
    for k in range(TOP_K):
        e_out = jnp.where(out_lane == k, idxs[k], e_out)
        p_out = jnp.where(out_lane == k, exps[k] / denom, p_out)
    e_ref[...] = e_out
    p_ref[...] = p_out


def _norm_mod(x2d, w, shift, scale, seq, *, tl, router=None):
    m, d = x2d.shape
    shift_a, shift_s = _mod_operand(shift, seq, tl)
    scale_a, scale_s = _mod_operand(scale, seq, tl)
    row_of = lambda i: i
    col_of = lambda i: 0
    in_specs = [pl.BlockSpec((tl, d), lambda i: (i, 0)),
                pl.BlockSpec((1, d), lambda i: (0, 0)),
                shift_s(d, row_of, col_of), scale_s(d, row_of, col_of)]
    args = [x2d, w.reshape(1, d), shift_a, scale_a]
    hn_spec = pl.BlockSpec((tl, d), lambda i: (i, 0))
    hn_shape = jax.ShapeDtypeStruct((m, d), BF16 if router is None else F32)
    if router is None:
        return pl.pallas_call(
            _norm_kernel, grid=(m // tl,), in_specs=in_specs, out_specs=hn_spec,
            out_shape=hn_shape, compiler_params=_cparams(1), name="ada_norm",
        )(*args)
    w_r, b_r = router
    n_e = w_r.shape[-1]
    in_specs += [pl.BlockSpec((d, n_e), lambda i: (0, 0)), pl.BlockSpec((1, n_e), lambda i: (0, 0))]
    args += [w_r, b_r.reshape(1, n_e)]
    return pl.pallas_call(
        _norm_router_kernel, grid=(m // tl,), in_specs=in_specs,
        out_specs=[hn_spec, pl.BlockSpec((tl, 128), lambda i: (i, 0)), pl.BlockSpec((tl, 128), lambda i: (i, 0))],
        out_shape=[hn_shape, jax.ShapeDtypeStruct((m, 128), I32), jax.ShapeDtypeStruct((m, 128), F32)],
        compiler_params=_cparams(1), name="ada_norm_router",
    )(*args)


def _conv_a_kernel(a_ref, g_ref, bglu_ref, hist_ref, wdw_ref, bdw_ref, lnw_ref, lnb_ref,
                   o_ref, newhist_ref, ubuf, vbuf, *, tl, width, cc):
    ch = a_ref.shape[-1]
    off = HIST_ROWS - (width - 1)

    @pl.when(pl.program_id(1) == 0)
    def _():
        ubuf[0:HIST_ROWS, :] = hist_ref[...]

    a = a_ref[...] + bglu_ref[:, 0:ch]
    g = g_ref[...] + bglu_ref[:, ch:2 * ch]
    ubuf[HIST_ROWS:HIST_ROWS + tl, :] = a * jax.nn.sigmoid(g)

    rc = min(tl, 128)
    for ri in range(tl // rc):
        r0 = ri * rc

        def col_body(ci, carry, r0=r0):
            c0 = pl.multiple_of(ci * cc, cc)
            acc = jnp.broadcast_to(bdw_ref[:, pl.ds(c0, cc)], (rc, cc))
            for k in range(width):
                acc = acc + wdw_ref[k:k + 1, pl.ds(c0, cc)] * ubuf[r0 + off + k:r0 + off + k + rc, pl.ds(c0, cc)]
            vbuf[r0:r0 + rc, pl.ds(c0, cc)] = acc
            return carry

        lax.fori_loop(0, ch // cc, col_body, 0)

        v = vbuf[r0:r0 + rc, :]
        mu = jnp.mean(v, axis=-1, keepdims=True)
        vc = v - mu
        var = jnp.mean(vc * vc, axis=-1, keepdims=True)
        y = vc * lax.rsqrt(var + LN_EPS) * lnw_ref[...] + lnb_ref[...]
        o_ref[r0:r0 + rc, :] = _silu(y).astype(o_ref.dtype)

    tail = ubuf[tl:tl + HIST_ROWS, :]
    newhist_ref[...] = tail
    ubuf[0:HIST_ROWS, :] = tail


def _conv_a(proj3, b_glu, hist, w_dw, b_dw, ln_w, ln_b, *, tl):
    bsz, seq, _ = proj3.shape
    width, ch = w_dw.shape
    cc = 256
    kern = functools.partial(_conv_a_kernel, tl=tl, width=width, cc=cc)
    full = lambda shape: pl.BlockSpec(shape, lambda b, t: (0,) * len(shape))
    return pl.pallas_call(
        kern, grid=(bsz, seq // tl),
        in_specs=[pl.BlockSpec((None, tl, ch), lambda b, t: (b, t, 0)),
                  pl.BlockSpec((None, tl, ch), lambda b, t: (b, t, 1)),
                  full((1, 2 * ch)),
                  pl.BlockSpec((None, HIST_ROWS, ch), lambda b, t: (b, 0, 0)),
                  full((width, ch)), full((1, ch)), full((1, ch)), full((1, ch))],
        out_specs=[pl.BlockSpec((None, tl, ch), lambda b, t: (b, t, 0)),
                   pl.BlockSpec((None, HIST_ROWS, ch), lambda b, t: (b, 0, 0))],
        out_shape=[jax.ShapeDtypeStruct((bsz, seq, ch), BF16),
                   jax.ShapeDtypeStruct((bsz, HIST_ROWS, ch), F32)],
        scratch_shapes=[pltpu.VMEM((HIST_ROWS + tl, ch), F32), pltpu.VMEM((tl, ch), F32)],
        compiler_params=_cparams(2), name="conformer_conv",
    )(proj3, proj3, b_glu.reshape(1, -1), hist, w_dw, b_dw.reshape(1, -1),
      ln_w.reshape(1, -1), ln_b.reshape(1, -1))


def _conv_b_kernel(xs_ref, bm_ref, cm_ref, dtr_ref, hx_ref, hb_ref, hc_ref, w_ref, b_ref, dtb_ref,
                   oxs_ref, obm_ref, ocm_ref, odt_ref, nhx_ref, nhb_ref, nhc_ref,
                   bx, bb, bc, *, tl, width, cc):
    off = HIST_ROWS - (width - 1)
    segs = ((xs_ref, hx_ref, oxs_ref, nhx_ref, bx), (bm_ref, hb_ref, obm_ref, nhb_ref, bb),
            (cm_ref, hc_ref, ocm_ref, nhc_ref, bc))
    first = pl.program_id(1) == 0
    rc = min(tl, 128)
    col0 = 0
    for in_ref, h_ref, out_ref, nh_ref, buf in segs:
        wseg = in_ref.shape[-1]

        @pl.when(first)
        def _(buf=buf, h_ref=h_ref):
            buf[0:HIST_ROWS, :] = h_ref[...]

        buf[HIST_ROWS:HIST_ROWS + tl, :] = in_ref[...]
        for ri in range(tl // rc):
            r0 = ri * rc

            def col_body(ci, carry, r0=r0, buf=buf, out_ref=out_ref, col0=col0):
                c0 = pl.multiple_of(ci * cc, cc)
                acc = jnp.broadcast_to(b_ref[:, pl.ds(col0 + c0, cc)], (rc, cc))
                for k in range(width):
                    acc = acc + w_ref[k:k + 1, pl.ds(col0 + c0, cc)] * buf[r0 + off + k:r0 + off + k + rc, pl.ds(c0, cc)]
                out_ref[r0:r0 + rc, pl.ds(c0, cc)] = _silu(acc)
                return carry

            lax.fori_loop(0, wseg // cc, col_body, 0)
        tail = buf[tl:tl + HIST_ROWS, :]
        nh_ref[...] = tail
        buf[0:HIST_ROWS, :] = tail
        col0 += wseg

    t = dtr_ref[...] + dtb_ref[...]
    odt_ref[...] = jnp.maximum(t, 0.0) + jnp.log1p(jnp.exp(-jnp.abs(t)))


def _conv_b(proj3, dt_raw3, hists, w_conv, b_conv, dt_bias, *, tl, d_ssm, col_xs):
    bsz, seq, _ = proj3.shape
    width, xbc = w_conv.shape
    gn = (xbc - d_ssm) // 2
    nh = dt_raw3.shape[-1]
    cc = 128
    kern = functools.partial(_conv_b_kernel, tl=tl, width=width, cc=cc)
    full = lambda shape: pl.BlockSpec(shape, lambda b, t: (0,) * len(shape))
    ix, ib, ic = col_xs // d_ssm, (col_xs + d_ssm) // gn, (col_xs + d_ssm + gn) // gn
    hspec = lambda w: pl.BlockSpec((None, HIST_ROWS, w), lambda b, t: (b, 0, 0))
    ospec = lambda w: pl.BlockSpec((None, tl, w), lambda b, t: (b, t, 0))
    return pl.pallas_call(
        kern, grid=(bsz, seq // tl),
        in_specs=[pl.BlockSpec((None, tl, d_ssm), lambda b, t: (b, t, ix)),
                  pl.BlockSpec((None, tl, gn), lambda b, t: (b, t, ib)),
                  pl.BlockSpec((None, tl, gn), lambda b, t: (b, t, ic)),
                  ospec(nh), hspec(d_ssm), hspec(gn), hspec(gn),
                  full((width, xbc)), full((1, xbc)), full((1, nh))],
        out_specs=[ospec(d_ssm), ospec(gn), ospec(gn), ospec(nh), hspec(d_ssm), hspec(gn), hspec(gn)],
        out_shape=[jax.ShapeDtypeStruct((bsz, seq, d_ssm), F32),
                   jax.ShapeDtypeStruct((bsz, seq, gn), F32),
                   jax.ShapeDtypeStruct((bsz, seq, gn), F32),
                   jax.ShapeDtypeStruct((bsz, seq, nh), F32),
                   jax.ShapeDtypeStruct((bsz, HIST_ROWS, d_ssm), F32),
                   jax.ShapeDtypeStruct((bsz, HIST_ROWS, gn), F32),
                   jax.ShapeDtypeStruct((bsz, HIST_ROWS, gn), F32)],
        scratch_shapes=[pltpu.VMEM((HIST_ROWS + tl, d_ssm), F32),
                        pltpu.VMEM((HIST_ROWS + tl, gn), F32),
                        pltpu.VMEM((HIST_ROWS + tl, gn), F32)],
        compiler_params=_cparams(2), name="ssm_conv",
    )(proj3, proj3, proj3, dt_raw3, *hists, w_conv, b_conv.reshape(1, -1), dt_bias.reshape(1, -1))


def _ssd_kernel(xs_ref, bm_ref, cm_ref, dt_ref, alog_ref, h0_ref, y_ref, hout_ref, h_scr, xdd_scr, *, q, hpg, gpb):
    c = pl.program_id(2)

    @pl.when(c == 0)
    def _():
        h_scr[...] = h0_ref[...]

    p = HEAD_DIM
    n = D_STATE
    gw = hpg * p
    row = lax.broadcasted_iota(I32, (q, q), 0)
    col = lax.broadcasted_iota(I32, (q, q), 1)
    causal = row >= col
    tril = causal.astype(F32)
    for gi in range(gpb):
        bmb = bm_ref[:, gi * n:(gi + 1) * n].astype(BF16)
        cmb = cm_ref[:, gi * n:(gi + 1) * n].astype(BF16)
        dt = dt_ref[gi]
        da = dt * (-jnp.exp(alog_ref[gi]))
        cum = jnp.dot(tril, da, precision=_HI, preferred_element_type=F32)
        cum_t = cum.T
        dt_t = dt.T
        cb = lax.dot_general(cmb, bmb, _NT, preferred_element_type=F32)
        cum_last = cum[q - 1:q, :]
        dd = jnp.exp(cum_last - cum) * dt
        ecum = jnp.exp(cum)
        cdec = jnp.exp(cum_last)
        hin = h_scr[gi]
        yo = lax.dot_general(cmb, hin.astype(BF16), _NT, preferred_element_type=F32)
        for r in range(hpg):
            c0 = gi * gw + r * p
            xr = xs_ref[:, c0:c0 + p]
            diff = cum[:, r:r + 1] - cum_t[r:r + 1, :]
            decay = jnp.exp(jnp.where(causal, diff, -jnp.inf))
            w = cb * decay * dt_t[r:r + 1, :]
            yd = jnp.dot(w.astype(BF16), xr.astype(BF16), preferred_element_type=F32)
            y_ref[:, c0:c0 + p] = yd + yo[:, r * p:(r + 1) * p] * ecum[:, r:r + 1]
            xdd_scr[:, c0:c0 + p] = xr * dd[:, r:r + 1]
        xdd = xdd_scr[:, gi * gw:(gi + 1) * gw].astype(BF16)
        st = lax.dot_general(xdd, bmb, _TN, preferred_element_type=F32)
        for r in range(hpg):
            rows = slice(r * p, (r + 1) * p)
            h_scr[gi, rows, :] = cdec[:, r:r + 1] * hin[rows, :] + st[rows, :]

    @pl.when(c == pl.num_programs(2) - 1)
    def _():
        hout_ref[...] = h_scr[...]


def _ssd(xs_c, bm_c, cm_c, dt_g, a_log_g, h0, *, q, gpb):
    bsz, seq, d_ssm = xs_c.shape
    n_groups = dt_g.shape[1]
    hpg = dt_g.shape[-1]
    gw = hpg * HEAD_DIM
    n = D_STATE
    kern = functools.partial(_ssd_kernel, q=q, hpg=hpg, gpb=gpb)
    y, h_fin = pl.pallas_call(
        kern, grid=(bsz, n_groups // gpb, seq // q),
        in_specs=[pl.BlockSpec((None, q, gpb * gw), lambda b, g, c: (b, c, g)),
                  pl.BlockSpec((None, q, gpb * n), lambda b, g, c: (b, c, g)),
                  pl.BlockSpec((None, q, gpb * n), lambda b, g, c: (b, c, g)),
                  pl.BlockSpec((None, gpb, q, hpg), lambda b, g, c: (b, g, c, 0)),
                  pl.BlockSpec((gpb, 1, hpg), lambda b, g, c: (g, 0, 0)),
                  pl.BlockSpec((None, gpb, gw, n), lambda b, g, c: (b, g, 0, 0))],
        out_specs=[pl.BlockSpec((None, q, gpb * gw), lambda b, g, c: (b, c, g)),
                   pl.BlockSpec((None, gpb, gw, n), lambda b, g, c: (b, g, 0, 0))],
        out_shape=[jax.ShapeDtypeStruct((bsz, seq, d_ssm), F32),
                   jax.ShapeDtypeStruct((bsz, n_groups, gw, n), F32)],
        scratch_shapes=[pltpu.VMEM((gpb, gw, n), F32), pltpu.VMEM((q, gpb * gw), F32)],
        compiler_params=_cparams(3), name="ssd_scan",
    )(xs_c, bm_c, cm_c, dt_g, a_log_g, h0.reshape(bsz, n_groups, gw, n))
    return y, h_fin.reshape(h0.shape)


def _gate_norm_kernel(y_ref, xs_ref, z_ref, dsk_ref, nw_ref, o_ref, *, gw):
    y = y_ref[...] + xs_ref[...] * dsk_ref[...]
    yz = y * _silu(z_ref[...])
    for g in range(y.shape[-1] // gw):
        seg = yz[:, g * gw:(g + 1) * gw]
        seg = seg * lax.rsqrt(jnp.mean(seg * seg, axis=-1, keepdims=True) + RMS_EPS)
        o_ref[:, g * gw:(g + 1) * gw] = (seg * nw_ref[:, g * gw:(g + 1) * gw]).astype(o_ref.dtype)


def _gate_norm(y2, xs2, proj2, d_skip_lanes, norm_w, *, tl, col_z, gw):
    m, d_ssm = y2.shape
    row = pl.BlockSpec((tl, d_ssm), lambda i: (i, 0))
    return pl.pallas_call(
        functools.partial(_gate_norm_kernel, gw=gw), grid=(m // tl,),
        in_specs=[row, row, pl.BlockSpec((tl, d_ssm), lambda i: (i, col_z // d_ssm)),
                  pl.BlockSpec((1, d_ssm), lambda i: (0, 0)), pl.BlockSpec((1, d_ssm), lambda i: (0, 0))],
        out_specs=row, out_shape=jax.ShapeDtypeStruct((m, d_ssm), BF16),
        compiler_params=_cparams(1), name="ssm_gate_norm",
    )(y2, xs2, proj2, d_skip_lanes, norm_w.reshape(1, -1))


def _out_proj_kernel(xa_ref, xb_ref, w_ref, x_ref, gate_ref, o_ref, wb_ref):
    @pl.when(pl.program_id(1) == 0)
    def _():
        _cast_rows(w_ref, wb_ref, 256)

    ka = xa_ref.shape[-1]
    kb = xb_ref.shape[-1]
    mixed = jnp.dot(xa_ref[...], wb_ref[0:ka, :], preferred_element_type=F32)
    mixed = mixed + jnp.dot(xb_ref[...], wb_ref[ka:ka + kb, :], preferred_element_type=F32)
    o_ref[...] = x_ref[...] + gate_ref[...] * mixed


def _out_proj(xa, xb, w_out, x2d, gate, seq, *, tm, tn):
    m, d = x2d.shape
    ka, kb = xa.shape[-1], xb.shape[-1]
    assert d % tn == 0 and m % tm == 0, (m, tm, d, tn)
    gate_a, gate_s = _mod_operand(gate, seq, tm)
    return pl.pallas_call(
        _out_proj_kernel, grid=(d // tn, m // tm),
        in_specs=[pl.BlockSpec((tm, ka), lambda j, i: (i, 0)),
                  pl.BlockSpec((tm, kb), lambda j, i: (i, 0)),
                  pl.BlockSpec((ka + kb, tn), lambda j, i: (0, j)),
                  pl.BlockSpec((tm, tn), lambda j, i: (i, j)),
                  gate_s(tn, lambda j, i: i, lambda j, i: j)],
        out_specs=pl.BlockSpec((tm, tn), lambda j, i: (i, j)),
        out_shape=jax.ShapeDtypeStruct((m, d), F32),
        scratch_shapes=[pltpu.VMEM((ka + kb, tn), BF16)],
        compiler_params=_cparams(2), name="out_proj",
    )(xa, xb, w_out, x2d, gate_a)


N_SEG_META = 5


def _moe_kernel(be_ref, nu_ref, first_ref, nxt_ref, lastseg_ref, x_ref, *rest, n_w, swiglu):
    w_hbm = rest[:n_w]
    b_refs = rest[n_w:2 * n_w]
    o_ref = rest[2 * n_w]
    wbuf, wb, sem = rest[2 * n_w + 1:]
    j = pl.program_id(0)
    i = pl.program_id(1)
    n_j = pl.num_programs(0)
    tf = wb.shape[-1]

    def w_copy(t, e, jj):
        cols = pl.ds(pl.multiple_of(jj * tf, tf), tf)
        return pltpu.make_async_copy(w_hbm[t].at[e, :, cols], wbuf.at[t], sem.at[t])

    @pl.when(i < nu_ref[0])
    def _():
        @pl.when(first_ref[i] == 1)
        def _():
            e = be_ref[i]

            @pl.when(jnp.logical_and(i == 0, j == 0))
            def _():
                for t in range(n_w):
                    w_copy(t, e, j).start()

            last = lastseg_ref[i]
            has_next = jnp.logical_not(jnp.logical_and(last == 1, j == n_j - 1))
            for t in range(n_w):
                w_copy(t, e, j).wait()
                _cast_rows(wbuf.at[t], wb.at[t], 256)

                @pl.when(has_next)
                def _(t=t):
                    w_copy(t, nxt_ref[i], j + last).start()

        x = x_ref[...]
        if swiglu:
            g = jnp.dot(x, wb[0], preferred_element_type=F32) + b_refs[0][...]
            u = jnp.dot(x, wb[1], preferred_element_type=F32) + b_refs[1][...]
            g = jnp.minimum(g, SWIGLU_LIMIT)
            u = jnp.clip(u, -SWIGLU_LIMIT, SWIGLU_LIMIT)
            o_ref[...] = ((u + 1.0) * (g * jax.nn.sigmoid(SWIGLU_ALPHA * g))).astype(o_ref.dtype)
        else:
            o_ref[...] = jnp.dot(x, wb[0], preferred_element_type=F32) + b_refs[0][...]

    @pl.when(i >= nu_ref[0])
    def _():
        o_ref[...] = jnp.zeros(o_ref.shape, o_ref.dtype)


def _moe_call(x, meta, weights, biases, *, tn, swiglu, out_dtype, name):
    rows, k = x.shape
    n_e, _, n_out = weights[0].shape
    n_w = len(weights)
    assert n_out % tn == 0 and rows % MOE_TM == 0, (rows, n_out, tn)
    nb = rows // MOE_TM
    vmem = (n_w * k * tn * 6 + 2 * MOE_TM * k * 2 + 2 * MOE_TM * tn * jnp.dtype(out_dtype).itemsize
            + MOE_SCRATCH_BYTES)
    row_of = lambda j, i, be, nu, *_: jnp.minimum(i, nu[0] - 1)
    bias_spec = pl.BlockSpec((None, 1, tn), lambda j, i, be, *_: (be[i], 0, j))
    return pl.pallas_call(
        functools.partial(_moe_kernel, n_w=n_w, swiglu=swiglu),
        grid_spec=pltpu.PrefetchScalarGridSpec(
            num_scalar_prefetch=N_SEG_META, grid=(n_out // tn, nb),
            in_specs=[pl.BlockSpec((MOE_TM, k), lambda j, i, *s: (row_of(j, i, *s), 0))]
            + [pl.BlockSpec(memory_space=pl.ANY)] * n_w + [bias_spec] * n_w,
            out_specs=pl.BlockSpec((MOE_TM, tn), lambda j, i, *s: (i, j)),
            scratch_shapes=[pltpu.VMEM((n_w, k, tn), F32), pltpu.VMEM((n_w, k, tn), BF16),
                            pltpu.SemaphoreType.DMA((n_w,))]),
        out_shape=jax.ShapeDtypeStruct((rows, n_out), out_dtype),
        compiler_params=_cparams(2, vmem), name=name,
    )(*meta, x, *weights, *[b.reshape(n_e, 1, n_out) for b in biases])


def _moe_ffn_rows(xg, meta, w_gate, b_gate, w_up, b_up, w_down, b_down, *, tf, tn):
    act = _moe_call(xg, meta, (w_gate, w_up), (b_gate, b_up), tn=tf, swiglu=True, out_dtype=BF16,
                    name="moe_gate_up")
    return _moe_call(act, meta, (w_down,), (b_down,), tn=tn, swiglu=False, out_dtype=F32, name="moe_down")


def _moe_dispatch(top_e, n_experts):
    n_tok = top_e.shape[0]
    n_assign = n_tok * TOP_K
    flat_e = top_e.reshape(-1)
    order = jnp.argsort(flat_e).astype(I32)
    rank = jnp.argsort(order).astype(I32)
    counts = jnp.sum((flat_e[:, None] == jnp.arange(n_experts, dtype=I32)[None, :]).astype(I32), axis=0)
    padded = (counts + MOE_TM - 1) // MOE_TM * MOE_TM
    pad_end = jnp.cumsum(padded)
    pad_start = pad_end - padded
    start = jnp.cumsum(counts) - counts
    pos = pad_start[flat_e] + rank - start[flat_e]
    nb = (n_assign + n_experts * (MOE_TM - 1) + MOE_TM - 1) // MOE_TM
    rows = jnp.arange(nb * MOE_TM, dtype=I32)
    row_e = jnp.minimum(jnp.sum((pad_end[None, :] <= (rows // MOE_TM * MOE_TM)[:, None]).astype(I32), axis=1),
                        n_experts - 1)
    off = rows - pad_start[row_e]
    live = jnp.logical_and(off < counts[row_e], rows < pad_end[-1])
    src = order[jnp.clip(start[row_e] + off, 0, n_assign - 1)] // TOP_K
    row_tok = jnp.where(live, src, 0).astype(I32)
    n_used = (pad_end[-1] // MOE_TM).astype(I32)
    idx = jnp.arange(nb, dtype=I32)
    blk = jnp.minimum(idx, n_used - 1) * MOE_TM
    blk_e = jnp.minimum(jnp.sum((pad_end[None, :] <= blk[:, None]).astype(I32), axis=1), n_experts - 1)
    prev_e = jnp.concatenate([jnp.full((1,), -1, I32), blk_e[:-1]])
    first = jnp.logical_and(idx < n_used, blk_e != prev_e)
    first_at = jnp.where(first, idx, nb)
    next_first = jnp.concatenate([lax.cummin(first_at, reverse=True)[1:], jnp.full((1,), nb, I32)])
    last_seg = next_first >= nb
    nxt_e = jnp.where(last_seg, blk_e[0], blk_e[jnp.minimum(next_first, nb - 1)])
    meta = (blk_e.astype(I32), n_used.reshape(1), first.astype(I32), nxt_e.astype(I32), last_seg.astype(I32))
    return row_tok, pos, meta


def _dispatch_kernel(tok_ref, nu_ref, src_hbm, o_ref, gbuf, sem):
    tm = o_ref.shape[0]
    i = pl.program_id(0)
    n_used = nu_ref[0]
    slot = lax.rem(i, 2)

    def row_copy(step, t, sl):
        r = tok_ref[step * tm + t]
        return pltpu.make_async_copy(src_hbm.at[pl.ds(r, 1), :], gbuf.at[sl, pl.ds(t, 1), :], sem.at[sl])

    def for_rows(step, sl, fn):
        def body(t, carry):
            fn(row_copy(step, t, sl))
            return carry

        lax.fori_loop(0, tm, body, 0, unroll=4)

    @pl.when(i == 0)
    def _():
        for_rows(0, 0, lambda cp: cp.start())

    @pl.when(i + 1 < n_used)
    def _():
        for_rows(i + 1, 1 - slot, lambda cp: cp.start())

    @pl.when(i < n_used)
    def _():
        for_rows(i, slot, lambda cp: cp.wait())
        o_ref[...] = gbuf[slot].astype(o_ref.dtype)

    @pl.when(i >= n_used)
    def _():
        o_ref[...] = jnp.zeros(o_ref.shape, o_ref.dtype)


def _dispatch_rows(hn, row_tok, n_used):
    rows = row_tok.shape[0]
    d = hn.shape[-1]
    return pl.pallas_call(
        _dispatch_kernel,
        grid_spec=pltpu.PrefetchScalarGridSpec(
            num_scalar_prefetch=2, grid=(rows // MOE_TM,),
            in_specs=[pl.BlockSpec(memory_space=pl.ANY)],
            out_specs=pl.BlockSpec((MOE_TM, d), lambda i, tok, nu: (i, 0)),
            scratch_shapes=[pltpu.VMEM((2, MOE_TM, d), hn.dtype), pltpu.SemaphoreType.DMA((2,))]),
        out_shape=jax.ShapeDtypeStruct((rows, d), BF16),
        compiler_params=_cparams(1), name="moe_dispatch",
    )(row_tok, n_used, hn)


def _combine_final_kernel(pos_ref, rows_hbm, p_ref, x_ref, gate_ref, w_ref, o_ref, gbuf, sem, *, tl):
    i = pl.program_id(0)
    n = pl.num_programs(0)
    slot = lax.rem(i, 2)

    def row_copy(step, t, k, sl):
        r = pos_ref[(step * tl + t) * TOP_K + k]
        return pltpu.make_async_copy(rows_hbm.at[pl.ds(r, 1), :], gbuf.at[sl, k, pl.ds(t, 1), :], sem.at[sl])

    def for_rows(step, sl, fn):
        def body(t, carry):
            for k in range(TOP_K):
                fn(row_copy(step, t, k, sl))
            return carry

        lax.fori_loop(0, tl, body, 0)

    @pl.when(i == 0)
    def _():
        for_rows(0, 0, lambda cp: cp.start())

    @pl.when(i + 1 < n)
    def _():
        for_rows(i + 1, 1 - slot, lambda cp: cp.start())

    for_rows(i, slot, lambda cp: cp.wait())

    probs = p_ref[...]
    ffn = probs[:, 0:1] * gbuf[slot, 0]
    for k in range(1, TOP_K):
        ffn = ffn + probs[:, k:k + 1] * gbuf[slot, k]
    x = x_ref[...] + gate_ref[...] * ffn
    o_ref[...] = x * lax.rsqrt(jnp.mean(x * x, axis=-1, keepdims=True) + RMS_EPS) * w_ref[...]


def _combine_final(x2d, rows_out, pos, probs, gate, w, seq, *, tl):
    m, d = x2d.shape
    gate_a, gate_s = _mod_operand(gate, seq, tl)
    row = pl.BlockSpec((tl, d), lambda i, pos: (i, 0))
    return pl.pallas_call(
        functools.partial(_combine_final_kernel, tl=tl),
        grid_spec=pltpu.PrefetchScalarGridSpec(
            num_scalar_prefetch=1, grid=(m // tl,),
            in_specs=[pl.BlockSpec(memory_space=pl.ANY),
                      pl.BlockSpec((tl, probs.shape[-1]), lambda i, pos: (i, 0)),
                      row, gate_s(d, lambda i, pos: i, lambda i, pos: 0),
                      pl.BlockSpec((1, d), lambda i, pos: (0, 0))],
            out_specs=row,
            scratch_shapes=[pltpu.VMEM((2, TOP_K, tl, d), F32), pltpu.SemaphoreType.DMA((2,))]),
        out_shape=jax.ShapeDtypeStruct((m, d), F32),
        compiler_params=_cparams(1), name="combine_final",
    )(pos, rows_out, probs, x2d, gate_a, w.reshape(1, d))


def _pad_hist(buf):
    return jnp.pad(buf, ((0, 0), (HIST_ROWS - buf.shape[1], 0), (0, 0)))


def _group_front(x, mod, buf_a, buf_b, ssm_h0, p, *, tl, tr, tm):
    bsz, seq, d = x.shape
    m = bsz * seq
    shift1, scale1, gate1, shift2, scale2, _ = jnp.split(mod, 6, axis=-1)
    conv_ch = p["w_dw"].shape[-1]
    d_ssm = p["ssm_norm_w"].shape[-1]
    n_heads = d_ssm // HEAD_DIM
    hpg = n_heads // SSM_GROUPS
    xbc = p["w_xbc_conv"].shape[-1]
    n_main = 2 * conv_ch + d_ssm + xbc
    x2d = x.reshape(m, d)

    hn1 = _norm_mod(x2d, p["norm1_w"], shift1, scale1, seq, tl=tr)
    proj = _mm(hn1, p["w_in"], jnp.zeros((1, n_main), F32), tm=tm, tn=1024 if n_main % 1024 == 0 else 512,
               n_out=n_main)
    dt_raw = _mm(hn1, p["w_in"][:, n_main:], jnp.zeros((1, n_heads), F32), tm=tm, tn=n_heads, n_out=n_heads)
    proj3 = proj.reshape(bsz, seq, n_main)

    out_a, hist_a = _conv_a(proj3, p["b_glu"], _pad_hist(buf_a), p["w_dw"], p["b_dw"], p["ln_w"], p["ln_b"], tl=tl)
    new_a = hist_a[:, HIST_ROWS - buf_a.shape[1]:]

    gn = (xbc - d_ssm) // 2
    hists = (_pad_hist(buf_b[..., :d_ssm]), _pad_hist(buf_b[..., d_ssm:d_ssm + gn]), _pad_hist(buf_b[..., d_ssm + gn:]))
    xs_c, bm_c, cm_c, dt, nhx, nhb, nhc = _conv_b(
        proj3, dt_raw.reshape(bsz, seq, n_heads), hists, p["w_xbc_conv"], p["b_xbc_conv"], p["dt_bias"],
        tl=tl, d_ssm=d_ssm, col_xs=2 * conv_ch + d_ssm)
    kb = buf_b.shape[1]
    new_b = jnp.concatenate([nhx, nhb, nhc], axis=-1)[:, HIST_ROWS - kb:]

    q = SSD_CHUNK if seq % SSD_CHUNK == 0 else seq
    dt_g = dt.reshape(bsz, seq, SSM_GROUPS, hpg).transpose(0, 2, 1, 3)
    gpb = SSM_GROUPS if q < SSD_CHUNK else min(4, SSM_GROUPS)
    y, h_fin = _ssd(xs_c, bm_c, cm_c, dt_g, p["a_log"].reshape(SSM_GROUPS, 1, hpg), ssm_h0, q=q, gpb=gpb)
    out_b = _gate_norm(y.reshape(m, d_ssm), xs_c.reshape(m, d_ssm), proj,
                       jnp.repeat(p["d_skip"], HEAD_DIM).reshape(1, d_ssm), p["ssm_norm_w"],
                       tl=tr, col_z=2 * conv_ch, gw=d_ssm // SSM_GROUPS)

    x1 = _out_proj(out_a.reshape(m, conv_ch), out_b, p["w_out"], x2d, gate1, seq, tm=tm, tn=512)
    hn2, top_e, probs = _norm_mod(x1, p["norm2_w"], shift2, scale2, seq, tl=tr,
                                  router=(p["w_router"], p["b_router"]))
    return x1, hn2, top_e[:, :TOP_K], probs, new_a, new_b, h_fin


def kernel(x_prompt, x_sample, state_conv_a, state_conv_ssm, state_ssm, c_prompt, c_sample, norm1_w, norm2_w, final_norm_w, w_ada, b_ada, w_in, b_glu, w_dw, b_dw, ln_w, ln_b, w_xbc_conv, b_xbc_conv, dt_bias, a_log, d_skip, ssm_norm_w, w_out, w_router, b_router, w_gate, b_gate, w_up, b_up, w_down, b_down):
    layer = 0
    p = dict(norm1_w=norm1_w[layer], norm2_w=norm2_w[layer], w_in=w_in[layer], b_glu=b_glu[layer],
             w_dw=w_dw[layer], b_dw=b_dw[layer], ln_w=ln_w[layer], ln_b=ln_b[layer],
             w_xbc_conv=w_xbc_conv[layer], b_xbc_conv=b_xbc_conv[layer], dt_bias=dt_bias[layer],
             a_log=a_log[layer], d_skip=d_skip[layer], ssm_norm_w=ssm_norm_w[layer], w_out=w_out[layer],
             w_router=w_router[layer], b_router=b_router[layer])
    bp, sp, d = x_prompt.shape
    bs, ss, _ = x_sample.shape
    mp, ms = bp * sp, bs * ss
    n_experts = w_router.shape[-1]

    c_all = jnp.concatenate([c_prompt, c_sample], axis=0)
    n_c = c_all.shape[0]
    n_cp = (n_c + 7) // 8 * 8
    c_all = jnp.pad(c_all, ((0, n_cp - n_c), (0, 0)))
    mod = _mm(c_all, w_ada[layer], b_ada[layer].reshape(1, -1), tm=n_cp, tn=512, n_out=w_ada.shape[-1], silu_in=True)

    zeros = lambda a, b: jnp.zeros((b,) + a.shape[2:], a.dtype)
    x1p, hn2p, ep, pp, conv_a_p, conv_b_p, ssm_p = _group_front(
        x_prompt, mod[:bp], zeros(state_conv_a, bp), zeros(state_conv_ssm, bp), zeros(state_ssm, bp), p,
        tl=min(256, sp), tr=min(256, mp), tm=min(512, mp))
    x1s, hn2s, es, ps, conv_a_s, conv_b_s, ssm_s = _group_front(
        x_sample, mod[bp:bp + bs], state_conv_a[layer], state_conv_ssm[layer], state_ssm[layer], p,
        tl=ss, tr=min(256, ms), tm=min(512, ms))

    hn2 = jnp.concatenate([hn2p, hn2s], axis=0)
    top_e = jnp.concatenate([ep, es], axis=0)
    row_tok, pos, meta = _moe_dispatch(top_e, n_experts)
    xg = _dispatch_rows(hn2, row_tok, meta[1])
    rows_out = _moe_ffn_rows(xg, meta, w_gate[layer], b_gate[layer], w_up[layer], b_up[layer],
                             w_down[layer], b_down[layer], tf=min(1024, w_gate.shape[-1]), tn=min(1024, d))

    gate2_p = mod[:bp, 5 * d:]
    gate2_s = mod[bp:bp + bs, 5 * d:]
    y_p = _combine_final(x1p, rows_out, pos[:mp * TOP_K], pp, gate2_p, final_norm_w, sp, tl=min(128, mp))
    y_s = _combine_final(x1s, rows_out, pos[mp * TOP_K:], ps, gate2_s, final_norm_w, ss, tl=min(128, ms))
    y_p = y_p.reshape(bp, sp, d)
    y_s = y_s.reshape(bs, ss, d)
    return (y_p, y_s, conv_a_p[None], conv_b_p[None], ssm_p[None], conv_a_s[None], conv_b_s[None], ssm_s[None])
```

```python
import functools

import jax
import jax.numpy as jnp
from jax import lax
from jax.experimental import pallas as pl
from jax.experimental.pallas import tpu as pltpu

F32 = jnp.float32
BF16 = jnp.bfloat16
I32 = jnp.int32

HEAD_DIM = 64
SSM_GROUPS = 8
D_STATE = 128
SSD_CHUNK = 128
TOP_K = 4
SWIGLU_LIMIT = 7.0
SWIGLU_ALPHA = 1.702
RMS_EPS = 1e-5
LN_EPS = 1e-5
HIST_ROWS = 32
VMEM_LIMIT = 56 * 1024 * 1024
MOE_TM = 256
MOE_SCRATCH_BYTES = 4 * 1024 * 1024

_HI = lax.Precision.HIGHEST
_NT = (((1,), (1,)), ((), ()))
_TN = (((0,), (0,)), ((), ()))


def _cparams(n_axes, vmem_bytes=VMEM_LIMIT):
    return pltpu.CompilerParams(dimension_semantics=("arbitrary",) * n_axes,
                                vmem_limit_bytes=vmem_bytes)


def _silu(x):
    return x * jax.nn.sigmoid(x)


def _cast_rows(src_ref, dst_ref, chunk):
    rows = src_ref.shape[0]

    def body(c, carry):
        r = pl.multiple_of(c * chunk, chunk)
        dst_ref[pl.ds(r, chunk), :] = src_ref[pl.ds(r, chunk), :].astype(BF16)
        return carry

    lax.fori_loop(0, rows // chunk, body, 0)


def _mm_kernel(x_ref, w_ref, b_ref, o_ref, wb_ref, *, silu_in):
    @pl.when(pl.program_id(1) == 0)
    def _():
        _cast_rows(w_ref, wb_ref, 256)

    x = x_ref[...]
    if silu_in:
        x = _silu(x.astype(F32))
    o_ref[...] = jnp.dot(x.astype(BF16), wb_ref[...], preferred_element_type=F32) + b_ref[...]


def _mm(x, w, b, *, tm, tn, n_out, silu_in=False):
    m, k = x.shape
    assert n_out % tn == 0 and m % tm == 0, (m, tm, n_out, tn)
    return pl.pallas_call(
        functools.partial(_mm_kernel, silu_in=silu_in),
        grid=(n_out // tn, m // tm),
        in_specs=[pl.BlockSpec((tm, k), lambda j, i: (i, 0)),
                  pl.BlockSpec((k, tn), lambda j, i: (0, j)),
                  pl.BlockSpec((1, tn), lambda j, i: (0, j))],
        out_specs=pl.BlockSpec((tm, tn), lambda j, i: (i, j)),
        out_shape=jax.ShapeDtypeStruct((m, n_out), F32),
        scratch_shapes=[pltpu.VMEM((k, tn), BF16)],
        compiler_params=_cparams(2),
        name="dense_mm",
    )(x, w, b)


def _mod_operand(v, seq, tile):
    bsz, d = v.shape
    if seq % tile == 0:
        per = seq // tile
        arr = v.reshape(bsz, 1, d)

        def spec(tn, row_of, col_of):
            return pl.BlockSpec((None, 1, tn), lambda *g: (row_of(*g) // per, 0, col_of(*g)))
    else:
        arr = jnp.repeat(v, seq, axis=0)

        def spec(tn, row_of, col_of):
            return pl.BlockSpec((tile, tn), lambda *g: (row_of(*g), col_of(*g)))
    return arr, spec


def _norm_kernel(x_ref, w_ref, shift_ref, scale_ref, o_ref):
    x = x_ref[...]
    y = x * lax.rsqrt(jnp.mean(x * x, axis=-1, keepdims=True) + RMS_EPS) * w_ref[...]
    o_ref[...] = (y * (1.0 + scale_ref[...]) + shift_ref[...]).astype(o_ref.dtype)


def _norm_router_kernel(x_ref, w_ref, shift_ref, scale_ref, wr_ref, br_ref, o_ref, e_ref, p_ref):
    x = x_ref[...]
    y = x * lax.rsqrt(jnp.mean(x * x, axis=-1, keepdims=True) + RMS_EPS) * w_ref[...]
    hn = y * (1.0 + scale_ref[...]) + shift_ref[...]
    o_ref[...] = hn.astype(o_ref.dtype)
    vals = jnp.dot(hn, wr_ref[...], precision=_HI, preferred_element_type=F32) + br_ref[...]
    n_e = vals.shape[-1]
    lane = lax.broadcasted_iota(I32, vals.shape, 1)
    tops, idxs = [], []
    for _ in range(TOP_K):
        m = jnp.max(vals, axis=-1, keepdims=True)
        idx = jnp.min(jnp.where(vals == m, lane, n_e), axis=-1, keepdims=True)
        tops.append(m)
        idxs.append(idx)
        vals = jnp.where(lane == idx, -jnp.inf, vals)
    exps = [jnp.exp(t - tops[0]) for t in tops]
    denom = exps[0]
    for ex in exps[1:]:
        denom = denom + ex
    out_lane = lax.broadcasted_iota(I32, e_ref.shape, 1)
    e_out = jnp.zeros(e_ref.shape, I32)
    p_out = jnp.zeros(p_ref.shape, F32)
    for k in range(TOP_K):
        e_out = jnp.where(out_lane == k, idxs[k], e_out)
        p_out = jnp.where(out_lane == k, exps[k] / denom, p_out)
    e_ref[...] = e_out
    p_ref[...] = p_out


def _norm_mod(x2d, w, shift, scale, seq, *, tl, router=None):
    m, d = x2d.shape
    shift_a, shift_s = _mod_operand(shift, seq, tl)
    scale_a, scale_s = _mod_operand(scale, seq, tl)
    row_of = lambda i: i
    col_of = lambda i: 0
    in_specs = [pl.BlockSpec((tl, d), lambda i: (i, 0)),
                pl.BlockSpec((1, d), lambda i: (0, 0)),
                shift_s(d, row_of, col_of), scale_s(d, row_of, col_of)]
    args = [x2d, w.reshape(1, d), shift_a, scale_a]
    hn_spec = pl.BlockSpec((tl, d), lambda i: (i, 0))
    hn_shape = jax.ShapeDtypeStruct((m, d), BF16 if router is None else F32)
    if router is None:
        return pl.pallas_call(
            _norm_kernel, grid=(m // tl,), in_specs=in_specs, out_specs=hn_spec,
            out_shape=hn_shape, compiler_params=_cparams(1), name="ada_norm",
        )(*args)
    w_r, b_r = router
    n_e = w_r.shape[-1]
    in_specs += [pl.BlockSpec((d, n_e), lambda i: (0, 0)), pl.BlockSpec((1, n_e), lambda i: (0, 0))]
    args += [w_r, b_r.reshape(1, n_e)]
    return pl.pallas_call(
        _norm_router_kernel, grid=(m // tl,), in_specs=in_specs,
        out_specs=[hn_spec, pl.BlockSpec((tl, 128), lambda i: (i, 0)), pl.BlockSpec((tl, 128), lambda i: (i, 0))],
        out_shape=[hn_shape, jax.ShapeDtypeStruct((m, 128), I32), jax.ShapeDtypeStruct((m, 128), F32)],
        compiler_params=_cparams(1), name="ada_norm_router",
    )(*args)


def _conv_a_kernel(a_ref, g_ref, bglu_ref, hist_ref, wdw_ref, bdw_ref, lnw_ref, lnb_ref,
                   o_ref, newhist_ref, ubuf, vbuf, *, sb, **kw):
    for s in range(sb):
        _conv_a_seq(a_ref.at[s], g_ref.at[s], bglu_ref, hist_ref.at[s], wdw_ref, bdw_ref, lnw_ref, lnb_ref,
                    o_ref.at[s], newhist_ref.at[s], ubuf, vbuf, **kw)


def _conv_a_seq(a_ref, g_ref, bglu_ref, hist_ref, wdw_ref, bdw_ref, lnw_ref, lnb_ref,
                o_ref, newhist_ref, ubuf, vbuf, *, tl, width, cc):
    ch = a_ref.shape[-1]
    off = HIST_ROWS - (width - 1)

    @pl.when(pl.program_id(1) == 0)
    def _():
        ubuf[0:HIST_ROWS, :] = hist_ref[...]

    a = a_ref[...] + bglu_ref[:, 0:ch]
    g = g_ref[...] + bglu_ref[:, ch:2 * ch]
    ubuf[HIST_ROWS:HIST_ROWS + tl, :] = a * jax.nn.sigmoid(g)

    rc = min(tl, 128)
    for ri in range(tl // rc):
        r0 = ri * rc

        def col_body(ci, carry, r0=r0):
            c0 = pl.multiple_of(ci * cc, cc)
            acc = jnp.broadcast_to(bdw_ref[:, pl.ds(c0, cc)], (rc, cc))
            for k in range(width):
                acc = acc + wdw_ref[k:k + 1, pl.ds(c0, cc)] * ubuf[r0 + off + k:r0 + off + k + rc, pl.ds(c0, cc)]
            vbuf[r0:r0 + rc, pl.ds(c0, cc)] = acc
            return carry

        lax.fori_loop(0, ch // cc, col_body, 0)

        v = vbuf[r0:r0 + rc, :]
        mu = jnp.mean(v, axis=-1, keepdims=True)
        vc = v - mu
        var = jnp.mean(vc * vc, axis=-1, keepdims=True)
        y = vc * lax.rsqrt(var + LN_EPS) * lnw_ref[...] + lnb_ref[...]
        o_ref[r0:r0 + rc, :] = _silu(y).astype(o_ref.dtype)

    tail = ubuf[tl:tl + HIST_ROWS, :]
    newhist_ref[...] = tail
    ubuf[0:HIST_ROWS, :] = tail


def _seqs_per_block(bsz, seq, tl):
    return 8 if (seq == tl and tl < 8 and bsz % 8 == 0) else 1


def _conv_a(proj3, b_glu, hist, w_dw, b_dw, ln_w, ln_b, *, tl):
    bsz, seq, _ = proj3.shape
    width, ch = w_dw.shape
    cc = 256
    sb = _seqs_per_block(bsz, seq, tl)
    kern = functools.partial(_conv_a_kernel, sb=sb, tl=tl, width=width, cc=cc)
    full = lambda shape: pl.BlockSpec(shape, lambda b, t: (0,) * len(shape))
    return pl.pallas_call(
        kern, grid=(bsz // sb, seq // tl),
        in_specs=[pl.BlockSpec((sb, tl, ch), lambda b, t: (b, t, 0)),
                  pl.BlockSpec((sb, tl, ch), lambda b, t: (b, t, 1)),
                  full((1, 2 * ch)),
                  pl.BlockSpec((sb, HIST_ROWS, ch), lambda b, t: (b, 0, 0)),
                  full((width, ch)), full((1, ch)), full((1, ch)), full((1, ch))],
        out_specs=[pl.BlockSpec((sb, tl, ch), lambda b, t: (b, t, 0)),
                   pl.BlockSpec((sb, HIST_ROWS, ch), lambda b, t: (b, 0, 0))],
        out_shape=[jax.ShapeDtypeStruct((bsz, seq, ch), BF16),
                   jax.ShapeDtypeStruct((bsz, HIST_ROWS, ch), F32)],
        scratch_shapes=[pltpu.VMEM((HIST_ROWS + tl, ch), F32), pltpu.VMEM((tl, ch), F32)],
        compiler_params=_cparams(2), name="conformer_conv",
    )(proj3, proj3, b_glu.reshape(1, -1), hist, w_dw, b_dw.reshape(1, -1),
      ln_w.reshape(1, -1), ln_b.reshape(1, -1))


def _conv_b_kernel(xs_ref, bm_ref, cm_ref, dtr_ref, hx_ref, hb_ref, hc_ref, w_ref, b_ref, dtb_ref,
                   oxs_ref, obm_ref, ocm_ref, odt_ref, nhx_ref, nhb_ref, nhc_ref,
                   bx, bb, bc, *, sb, **kw):
    for s in range(sb):
        _conv_b_seq(xs_ref.at[s], bm_ref.at[s], cm_ref.at[s], dtr_ref.at[s], hx_ref.at[s], hb_ref.at[s],
                    hc_ref.at[s], w_ref, b_ref, dtb_ref, oxs_ref.at[s], obm_ref.at[s], ocm_ref.at[s],
                    odt_ref.at[s], nhx_ref.at[s], nhb_ref.at[s], nhc_ref.at[s], bx, bb, bc, **kw)


def _conv_b_seq(xs_ref, bm_ref, cm_ref, dtr_ref, hx_ref, hb_ref, hc_ref, w_ref, b_ref, dtb_ref,
                oxs_ref, obm_ref, ocm_ref, odt_ref, nhx_ref, nhb_ref, nhc_ref,
                bx, bb, bc, *, tl, width, cc):
    off = HIST_ROWS - (width - 1)
    segs = ((xs_ref, hx_ref, oxs_ref, nhx_ref, bx), (bm_ref, hb_ref, obm_ref, nhb_ref, bb),
            (cm_ref, hc_ref, ocm_ref, nhc_ref, bc))
    first = pl.program_id(1) == 0
    rc = min(tl, 128)
    col0 = 0
    for in_ref, h_ref, out_ref, nh_ref, buf in segs:
        wseg = in_ref.shape[-1]

        @pl.when(first)
        def _(buf=buf, h_ref=h_ref):
            buf[0:HIST_ROWS, :] = h_ref[...]

        buf[HIST_ROWS:HIST_ROWS + tl, :] = in_ref[...]
        for ri in range(tl // rc):
            r0 = ri * rc

            def col_body(ci, carry, r0=r0, buf=buf, out_ref=out_ref, col0=col0):
                c0 = pl.multiple_of(ci * cc, cc)
                acc = jnp.broadcast_to(b_ref[:, pl.ds(col0 + c0, cc)], (rc, cc))
                for k in range(width):
                    acc = acc + w_ref[k:k + 1, pl.ds(col0 + c0, cc)] * buf[r0 + off + k:r0 + off + k + rc, pl.ds(c0, cc)]
                out_ref[r0:r0 + rc, pl.ds(c0, cc)] = _silu(acc)
                return carry

            lax.fori_loop(0, wseg // cc, col_body, 0)
        tail = buf[tl:tl + HIST_ROWS, :]
        nh_ref[...] = tail
        buf[0:HIST_ROWS, :] = tail
        col0 += wseg

    t = dtr_ref[...] + dtb_ref[...]
    odt_ref[...] = jnp.maximum(t, 0.0) + jnp.log1p(jnp.exp(-jnp.abs(t)))


def _conv_b(proj3, dt_raw3, hists, w_conv, b_conv, dt_bias, *, tl, d_ssm, col_xs):
    bsz, seq, _ = proj3.shape
    width, xbc = w_conv.shape
    gn = (xbc - d_ssm) // 2
    nh = dt_raw3.shape[-1]
    cc = 128
    sb = _seqs_per_block(bsz, seq, tl)
    kern = functools.partial(_conv_b_kernel, sb=sb, tl=tl, width=width, cc=cc)
    full = lambda shape: pl.BlockSpec(shape, lambda b, t: (0,) * len(shape))
    ix, ib, ic = col_xs // d_ssm, (col_xs + d_ssm) // gn, (col_xs + d_ssm + gn) // gn
    hspec = lambda w: pl.BlockSpec((sb, HIST_ROWS, w), lambda b, t: (b, 0, 0))
    ospec = lambda w: pl.BlockSpec((sb, tl, w), lambda b, t: (b, t, 0))
    return pl.pallas_call(
        kern, grid=(bsz // sb, seq // tl),
        in_specs=[pl.BlockSpec((sb, tl, d_ssm), lambda b, t: (b, t, ix)),
                  pl.BlockSpec((sb, tl, gn), lambda b, t: (b, t, ib)),
                  pl.BlockSpec((sb, tl, gn), lambda b, t: (b, t, ic)),
                  ospec(nh), hspec(d_ssm), hspec(gn), hspec(gn),
                  full((width, xbc)), full((1, xbc)), full((1, nh))],
        out_specs=[ospec(d_ssm), ospec(gn), ospec(gn), ospec(nh), hspec(d_ssm), hspec(gn), hspec(gn)],
        out_shape=[jax.ShapeDtypeStruct((bsz, seq, d_ssm), F32),
                   jax.ShapeDtypeStruct((bsz, seq, gn), F32),
                   jax.ShapeDtypeStruct((bsz, seq, gn), F32),
                   jax.ShapeDtypeStruct((bsz, seq, nh), F32),
                   jax.ShapeDtypeStruct((bsz, HIST_ROWS, d_ssm), F32),
                   jax.ShapeDtypeStruct((bsz, HIST_ROWS, gn), F32),
                   jax.ShapeDtypeStruct((bsz, HIST_ROWS, gn), F32)],
        scratch_shapes=[pltpu.VMEM((HIST_ROWS + tl, d_ssm), F32),
                        pltpu.VMEM((HIST_ROWS + tl, gn), F32),
                        pltpu.VMEM((HIST_ROWS + tl, gn), F32)],
        compiler_params=_cparams(2), name="ssm_conv",
    )(proj3, proj3, proj3, dt_raw3, *hists, w_conv, b_conv.reshape(1, -1), dt_bias.reshape(1, -1))


def _ssd_kernel(xs_ref, bm_ref, cm_ref, dt_ref, alog_ref, h0_ref, y_ref, hout_ref, h_scr, xdd_scr, *, q, hpg, gpb):
    c = pl.program_id(2)

    @pl.when(c == 0)
    def _():
        h_scr[...] = h0_ref[...]

    p = HEAD_DIM
    n = D_STATE
    gw = hpg * p
    row = lax.broadcasted_iota(I32, (q, q), 0)
    col = lax.broadcasted_iota(I32, (q, q), 1)
    causal = row >= col
    tril = causal.astype(F32)
    for gi in range(gpb):
        bmb = bm_ref[:, gi * n:(gi + 1) * n].astype(BF16)
        cmb = cm_ref[:, gi * n:(gi + 1) * n].astype(BF16)
        dt = dt_ref[gi]
        da = dt * (-jnp.exp(alog_ref[gi]))
        cum = jnp.dot(tril, da, precision=_HI, preferred_element_type=F32)
        cum_t = cum.T
        dt_t = dt.T
        cb = lax.dot_general(cmb, bmb, _NT, preferred_element_type=F32)
        cum_last = cum[q - 1:q, :]
        dd = jnp.exp(cum_last - cum) * dt
        ecum = jnp.exp(cum)
        cdec = jnp.exp(cum_last)
        hin = h_scr[gi]
        yo = lax.dot_general(cmb, hin.astype(BF16), _NT, preferred_element_type=F32)
        for r in range(hpg):
            c0 = gi * gw + r * p
            xr = xs_ref[:, c0:c0 + p]
            diff = cum[:, r:r + 1] - cum_t[r:r + 1, :]
            decay = jnp.exp(jnp.where(causal, diff, -jnp.inf))
            w = cb * decay * dt_t[r:r + 1, :]
            yd = jnp.dot(w.astype(BF16), xr.astype(BF16), preferred_element_type=F32)
            y_ref[:, c0:c0 + p] = yd + yo[:, r * p:(r + 1) * p] * ecum[:, r:r + 1]
            xdd_scr[:, c0:c0 + p] = xr * dd[:, r:r + 1]
        xdd = xdd_scr[:, gi * gw:(gi + 1) * gw].astype(BF16)
        st = lax.dot_general(xdd, bmb, _TN, preferred_element_type=F32)
        for r in range(hpg):
            rows = slice(r * p, (r + 1) * p)
            h_scr[gi, rows, :] = cdec[:, r:r + 1] * hin[rows, :] + st[rows, :]

    @pl.when(c == pl.num_programs(2) - 1)
    def _():
        hout_ref[...] = h_scr[...]


def _ssd(xs_c, bm_c, cm_c, dt_g, a_log_g, h0, *, q, gpb):
    bsz, seq, d_ssm = xs_c.shape
    n_groups = dt_g.shape[1]
    hpg = dt_g.shape[-1]
    gw = hpg * HEAD_DIM
    n = D_STATE
    kern = functools.partial(_ssd_kernel, q=q, hpg=hpg, gpb=gpb)
    y, h_fin = pl.pallas_call(
        kern, grid=(bsz, n_groups // gpb, seq // q),
        in_specs=[pl.BlockSpec((None, q, gpb * gw), lambda b, g, c: (b, c, g)),
                  pl.BlockSpec((None, q, gpb * n), lambda b, g, c: (b, c, g)),
                  pl.BlockSpec((None, q, gpb * n), lambda b, g, c: (b, c, g)),
                  pl.BlockSpec((None, gpb, q, hpg), lambda b, g, c: (b, g, c, 0)),
                  pl.BlockSpec((gpb, 1, hpg), lambda b, g, c: (g, 0, 0)),
                  pl.BlockSpec((None, gpb, gw, n), lambda b, g, c: (b, g, 0, 0))],
        out_specs=[pl.BlockSpec((None, q, gpb * gw), lambda b, g, c: (b, c, g)),
                   pl.BlockSpec((None, gpb, gw, n), lambda b, g, c: (b, g, 0, 0))],
        out_shape=[jax.ShapeDtypeStruct((bsz, seq, d_ssm), F32),
                   jax.ShapeDtypeStruct((bsz, n_groups, gw, n), F32)],
        scratch_shapes=[pltpu.VMEM((gpb, gw, n), F32), pltpu.VMEM((q, gpb * gw), F32)],
        compiler_params=_cparams(3), name="ssd_scan",
    )(xs_c, bm_c, cm_c, dt_g, a_log_g, h0.reshape(bsz, n_groups, gw, n))
    return y, h_fin.reshape(h0.shape)


def _gate_norm_kernel(y_ref, xs_ref, z_ref, dsk_ref, nw_ref, o_ref, *, gw):
    y = y_ref[...] + xs_ref[...] * dsk_ref[...]
    yz = y * _silu(z_ref[...])
    for g in range(y.shape[-1] // gw):
        seg = yz[:, g * gw:(g + 1) * gw]
        seg = seg * lax.rsqrt(jnp.mean(seg * seg, axis=-1, keepdims=True) + RMS_EPS)
        o_ref[:, g * gw:(g + 1) * gw] = (seg * nw_ref[:, g * gw:(g + 1) * gw]).astype(o_ref.dtype)


def _gate_norm(y2, xs2, proj2, d_skip_lanes, norm_w, *, tl, col_z, gw):
    m, d_ssm = y2.shape
    row = pl.BlockSpec((tl, d_ssm), lambda i: (i, 0))
    return pl.pallas_call(
        functools.partial(_gate_norm_kernel, gw=gw), grid=(m // tl,),
        in_specs=[row, row, pl.BlockSpec((tl, d_ssm), lambda i: (i, col_z // d_ssm)),
                  pl.BlockSpec((1, d_ssm), lambda i: (0, 0)), pl.BlockSpec((1, d_ssm), lambda i: (0, 0))],
        out_specs=row, out_shape=jax.ShapeDtypeStruct((m, d_ssm), BF16),
        compiler_params=_cparams(1), name="ssm_gate_norm",
    )(y2, xs2, proj2, d_skip_lanes, norm_w.reshape(1, -1))


def _out_proj_kernel(xa_ref, xb_ref, w_ref, x_ref, gate_ref, o_ref, wb_ref):
    @pl.when(pl.program_id(1) == 0)
    def _():
        _cast_rows(w_ref, wb_ref, 256)

    ka = xa_ref.shape[-1]
    kb = xb_ref.shape[-1]
    mixed = jnp.dot(xa_ref[...], wb_ref[0:ka, :], preferred_element_type=F32)
    mixed = mixed + jnp.dot(xb_ref[...], wb_ref[ka:ka + kb, :], preferred_element_type=F32)
    o_ref[...] = x_ref[...] + gate_ref[...] * mixed


def _out_proj(xa, xb, w_out, x2d, gate, seq, *, tm, tn):
    m, d = x2d.shape
    ka, kb = xa.shape[-1], xb.shape[-1]
    assert d % tn == 0 and m % tm == 0, (m, tm, d, tn)
    gate_a, gate_s = _mod_operand(gate, seq, tm)
    return pl.pallas_call(
        _out_proj_kernel, grid=(d // tn, m // tm),
        in_specs=[pl.BlockSpec((tm, ka), lambda j, i: (i, 0)),
                  pl.BlockSpec((tm, kb), lambda j, i: (i, 0)),
                  pl.BlockSpec((ka + kb, tn), lambda j, i: (0, j)),
                  pl.BlockSpec((tm, tn), lambda j, i: (i, j)),
                  gate_s(tn, lambda j, i: i, lambda j, i: j)],
        out_specs=pl.BlockSpec((tm, tn), lambda j, i: (i, j)),
        out_shape=jax.ShapeDtypeStruct((m, d), F32),
        scratch_shapes=[pltpu.VMEM((ka + kb, tn), BF16)],
        compiler_params=_cparams(2), name="out_proj",
    )(xa, xb, w_out, x2d, gate_a)


N_SEG_META = 5


def _moe_kernel(be_ref, nu_ref, first_ref, nxt_ref, lastseg_ref, x_ref, *rest, n_w, swiglu):
    w_hbm = rest[:n_w]
    b_refs = rest[n_w:2 * n_w]
    o_ref = rest[2 * n_w]
    wbuf, wb, sem = rest[2 * n_w + 1:]
    j = pl.program_id(0)
    i = pl.program_id(1)
    n_j = pl.num_programs(0)
    tf = wb.shape[-1]

    def w_copy(t, e, jj):
        cols = pl.ds(pl.multiple_of(jj * tf, tf), tf)
        return pltpu.make_async_copy(w_hbm[t].at[e, :, cols], wbuf.at[t], sem.at[t])

    @pl.when(i < nu_ref[0])
    def _():
        @pl.when(first_ref[i] == 1)
        def _():
            e = be_ref[i]

            @pl.when(jnp.logical_and(i == 0, j == 0))
            def _():
                for t in range(n_w):
                    w_copy(t, e, j).start()

            last = lastseg_ref[i]
            has_next = jnp.logical_not(jnp.logical_and(last == 1, j == n_j - 1))
            for t in range(n_w):
                w_copy(t, e, j).wait()
                _cast_rows(wbuf.at[t], wb.at[t], 256)

                @pl.when(has_next)
                def _(t=t):
                    w_copy(t, nxt_ref[i], j + last).start()

        x = x_ref[...]
        if swiglu:
            g = jnp.dot(x, wb[0], preferred_element_type=F32) + b_refs[0][...]
            u = jnp.dot(x, wb[1], preferred_element_type=F32) + b_refs[1][...]
            g = jnp.minimum(g, SWIGLU_LIMIT)
            u = jnp.clip(u, -SWIGLU_LIMIT, SWIGLU_LIMIT)
            o_ref[...] = ((u + 1.0) * (g * jax.nn.sigmoid(SWIGLU_ALPHA * g))).astype(o_ref.dtype)
        else:
            o_ref[...] = jnp.dot(x, wb[0], preferred_element_type=F32) + b_refs[0][...]

    @pl.when(i >= nu_ref[0])
    def _():
        o_ref[...] = jnp.zeros(o_ref.shape, o_ref.dtype)


def _moe_call(x, meta, weights, biases, *, tn, swiglu, out_dtype, name):
    rows, k = x.shape
    n_e, _, n_out = weights[0].shape
    n_w = len(weights)
    assert n_out % tn == 0 and rows % MOE_TM == 0, (rows, n_out, tn)
    nb = rows // MOE_TM
    vmem = (n_w * k * tn * 6 + 2 * MOE_TM * k * 2 + 2 * MOE_TM * tn * jnp.dtype(out_dtype).itemsize
            + MOE_SCRATCH_BYTES)
    row_of = lambda j, i, be, nu, *_: jnp.minimum(i, nu[0] - 1)
    bias_spec = pl.BlockSpec((None, 1, tn), lambda j, i, be, *_: (be[i], 0, j))
    return pl.pallas_call(
        functools.partial(_moe_kernel, n_w=n_w, swiglu=swiglu),
        grid_spec=pltpu.PrefetchScalarGridSpec(
            num_scalar_prefetch=N_SEG_META, grid=(n_out // tn, nb),
            in_specs=[pl.BlockSpec((MOE_TM, k), lambda j, i, *s: (row_of(j, i, *s), 0))]
            + [pl.BlockSpec(memory_space=pl.ANY)] * n_w + [bias_spec] * n_w,
            out_specs=pl.BlockSpec((MOE_TM, tn), lambda j, i, *s: (i, j)),
            scratch_shapes=[pltpu.VMEM((n_w, k, tn), F32), pltpu.VMEM((n_w, k, tn), BF16),
                            pltpu.SemaphoreType.DMA((n_w,))]),
        out_shape=jax.ShapeDtypeStruct((rows, n_out), out_dtype),
        compiler_params=_cparams(2, vmem), name=name,
    )(*meta, x, *weights, *[b.reshape(n_e, 1, n_out) for b in biases])


def _moe_ffn_rows(xg, meta, w_gate, b_gate, w_up, b_up, w_down, b_down, *, tf, tn):
    act = _moe_call(xg, meta, (w_gate, w_up), (b_gate, b_up), tn=tf, swiglu=True, out_dtype=BF16,
                    name="moe_gate_up")
    return _moe_call(act, meta, (w_down,), (b_down,), tn=tn, swiglu=False, out_dtype=F32, name="moe_down")


def _moe_dispatch(top_e, n_experts):
    n_tok = top_e.shape[0]
    n_assign = n_tok * TOP_K
    flat_e = top_e.reshape(-1)
    order = jnp.argsort(flat_e).astype(I32)
    rank = jnp.argsort(order).astype(I32)
    counts = jnp.sum((flat_e[:, None] == jnp.arange(n_experts, dtype=I32)[None, :]).astype(I32), axis=0)
    padded = (counts + MOE_TM - 1) // MOE_TM * MOE_TM
    pad_end = jnp.cumsum(padded)
    pad_start = pad_end - padded
    start = jnp.cumsum(counts) - counts
    pos = pad_start[flat_e] + rank - start[flat_e]
    nb = (n_assign + n_experts * (MOE_TM - 1) + MOE_TM - 1) // MOE_TM
    rows = jnp.arange(nb * MOE_TM, dtype=I32)
    row_e = jnp.minimum(jnp.sum((pad_end[None, :] <= (rows // MOE_TM * MOE_TM)[:, None]).astype(I32), axis=1),
                        n_experts - 1)
    off = rows - pad_start[row_e]
    live = jnp.logical_and(off < counts[row_e], rows < pad_end[-1])
    src = order[jnp.clip(start[row_e] + off, 0, n_assign - 1)] // TOP_K
    row_tok = jnp.where(live, src, 0).astype(I32)
    n_used = (pad_end[-1] // MOE_TM).astype(I32)
    idx = jnp.arange(nb, dtype=I32)
    blk = jnp.minimum(idx, n_used - 1) * MOE_TM
    blk_e = jnp.minimum(jnp.sum((pad_end[None, :] <= blk[:, None]).astype(I32), axis=1), n_experts - 1)
    prev_e = jnp.concatenate([jnp.full((1,), -1, I32), blk_e[:-1]])
    first = jnp.logical_and(idx < n_used, blk_e != prev_e)
    first_at = jnp.where(first, idx, nb)
    next_first = jnp.concatenate([lax.cummin(first_at, reverse=True)[1:], jnp.full((1,), nb, I32)])
    last_seg = next_first >= nb
    nxt_e = jnp.where(last_seg, blk_e[0], blk_e[jnp.minimum(next_first, nb - 1)])
    meta = (blk_e.astype(I32), n_used.reshape(1), first.astype(I32), nxt_e.astype(I32), last_seg.astype(I32))
    return row_tok, pos, meta


def _dispatch_kernel(tok_ref, nu_ref, src_hbm, o_ref, gbuf, sem):
    tm = o_ref.shape[0]
    i = pl.program_id(0)
    n_used = nu_ref[0]
    slot = lax.rem(i, 2)

    def row_copy(step, t, sl):
        r = tok_ref[step * tm + t]
        return pltpu.make_async_copy(src_hbm.at[pl.ds(r, 1), :], gbuf.at[sl, pl.ds(t, 1), :], sem.at[sl])

    def for_rows(step, sl, fn):
        def body(u, carry):
            for h in range(2):
                fn(row_copy(step, 2 * u + h, sl), h)
            return carry

        lax.fori_loop(0, tm // 2, body, 0, unroll=2)

    @pl.when(i == 0)
    def _():
        for_rows(0, 0, lambda cp, h: cp.start(priority=h))

    @pl.when(i + 1 < n_used)
    def _():
        for_rows(i + 1, 1 - slot, lambda cp, h: cp.start(priority=h))

    @pl.when(i < n_used)
    def _():
        for_rows(i, slot, lambda cp, h: cp.wait())
        o_ref[...] = gbuf[slot].astype(o_ref.dtype)

    @pl.when(i >= n_used)
    def _():
        o_ref[...] = jnp.zeros(o_ref.shape, o_ref.dtype)


def _dispatch_rows(hn, row_tok, n_used):
    rows = row_tok.shape[0]
    d = hn.shape[-1]
    return pl.pallas_call(
        _dispatch_kernel,
        grid_spec=pltpu.PrefetchScalarGridSpec(
            num_scalar_prefetch=2, grid=(rows // MOE_TM,),
            in_specs=[pl.BlockSpec(memory_space=pl.ANY)],
            out_specs=pl.BlockSpec((MOE_TM, d), lambda i, tok, nu: (i, 0)),
            scratch_shapes=[pltpu.VMEM((2, MOE_TM, d), hn.dtype), pltpu.SemaphoreType.DMA((2,))]),
        out_shape=jax.ShapeDtypeStruct((rows, d), BF16),
        compiler_params=_cparams(1), name="moe_dispatch",
    )(row_tok, n_used, hn)


def _combine_final_kernel(pos_ref, rows_hbm, p_ref, x_ref, gate_ref, w_ref, o_ref, gbuf, sem, *, tl):
    i = pl.program_id(0)
    n = pl.num_programs(0)
    slot = lax.rem(i, 2)

    def row_copy(step, t, k, sl):
        r = pos_ref[(step * tl + t) * TOP_K + k]
        return pltpu.make_async_copy(rows_hbm.at[pl.ds(r, 1), :], gbuf.at[sl, k, pl.ds(t, 1), :], sem.at[sl])

    def for_rows(step, sl, fn):
        def body(t, carry):
            for k in range(TOP_K):
                fn(row_copy(step, t, k, sl), k % 2)
            return carry

        lax.fori_loop(0, tl, body, 0)

    @pl.when(i == 0)
    def _():
        for_rows(0, 0, lambda cp, h: cp.start(priority=h))

    @pl.when(i + 1 < n)
    def _():
        for_rows(i + 1, 1 - slot, lambda cp, h: cp.start(priority=h))

    for_rows(i, slot, lambda cp, h: cp.wait())

    probs = p_ref[...]
    ffn = probs[:, 0:1] * gbuf[slot, 0]
    for k in range(1, TOP_K):
        ffn = ffn + probs[:, k:k + 1] * gbuf[slot, k]
    x = x_ref[...] + gate_ref[...] * ffn
    o_ref[...] = x * lax.rsqrt(jnp.mean(x * x, axis=-1, keepdims=True) + RMS_EPS) * w_ref[...]


def _combine_final(x2d, rows_out, pos, probs, gate, w, seq, *, tl):
    m, d = x2d.shape
    gate_a, gate_s = _mod_operand(gate, seq, tl)
    row = pl.BlockSpec((tl, d), lambda i, pos: (i, 0))
    return pl.pallas_call(
        functools.partial(_combine_final_kernel, tl=tl),
        grid_spec=pltpu.PrefetchScalarGridSpec(
            num_scalar_prefetch=1, grid=(m // tl,),
            in_specs=[pl.BlockSpec(memory_space=pl.ANY),
                      pl.BlockSpec((tl, probs.shape[-1]), lambda i, pos: (i, 0)),
                      row, gate_s(d, lambda i, pos: i, lambda i, pos: 0),
                      pl.BlockSpec((1, d), lambda i, pos: (0, 0))],
            out_specs=row,
            scratch_shapes=[pltpu.VMEM((2, TOP_K, tl, d), F32), pltpu.SemaphoreType.DMA((2,))]),
        out_shape=jax.ShapeDtypeStruct((m, d), F32),
        compiler_params=_cparams(1), name="combine_final",
    )(pos, rows_out, probs, x2d, gate_a, w.reshape(1, d))


def _pad_hist(buf):
    return jnp.pad(buf, ((0, 0), (HIST_ROWS - buf.shape[1], 0), (0, 0)))


def _group_front(x, mod, buf_a, buf_b, ssm_h0, p, *, tl, tr, tm):
    bsz, seq, d = x.shape
    m = bsz * seq
    shift1, scale1, gate1, shift2, scale2, _ = jnp.split(mod, 6, axis=-1)
    conv_ch = p["w_dw"].shape[-1]
    d_ssm = p["ssm_norm_w"].shape[-1]
    n_heads = d_ssm // HEAD_DIM
    hpg = n_heads // SSM_GROUPS
    xbc = p["w_xbc_conv"].shape[-1]
    n_main = 2 * conv_ch + d_ssm + xbc
    x2d = x.reshape(m, d)

    hn1 = _norm_mod(x2d, p["norm1_w"], shift1, scale1, seq, tl=tr)
    proj = _mm(hn1, p["w_in"], jnp.zeros((1, n_main), F32), tm=tm, tn=1024 if n_main % 1024 == 0 else 512,
               n_out=n_main)
    dt_raw = _mm(hn1, p["w_in"][:, n_main:], jnp.zeros((1, n_heads), F32), tm=tm, tn=n_heads, n_out=n_heads)
    proj3 = proj.reshape(bsz, seq, n_main)

    out_a, hist_a = _conv_a(proj3, p["b_glu"], _pad_hist(buf_a), p["w_dw"], p["b_dw"], p["ln_w"], p["ln_b"], tl=tl)
    new_a = hist_a[:, HIST_ROWS - buf_a.shape[1]:]

    gn = (xbc - d_ssm) // 2
    hists = (_pad_hist(buf_b[..., :d_ssm]), _pad_hist(buf_b[..., d_ssm:d_ssm + gn]), _pad_hist(buf_b[..., d_ssm + gn:]))
    xs_c, bm_c, cm_c, dt, nhx, nhb, nhc = _conv_b(
        proj3, dt_raw.reshape(bsz, seq, n_heads), hists, p["w_xbc_conv"], p["b_xbc_conv"], p["dt_bias"],
        tl=tl, d_ssm=d_ssm, col_xs=2 * conv_ch + d_ssm)
    kb = buf_b.shape[1]
    new_b = jnp.concatenate([nhx, nhb, nhc], axis=-1)[:, HIST_ROWS - kb:]

    q = SSD_CHUNK if seq % SSD_CHUNK == 0 else seq
    dt_g = dt.reshape(bsz, seq, SSM_GROUPS, hpg).transpose(0, 2, 1, 3)
    gpb = SSM_GROUPS if q < SSD_CHUNK else min(4, SSM_GROUPS)
    y, h_fin = _ssd(xs_c, bm_c, cm_c, dt_g, p["a_log"].reshape(SSM_GROUPS, 1, hpg), ssm_h0, q=q, gpb=gpb)
    out_b = _gate_norm(y.reshape(m, d_ssm), xs_c.reshape(m, d_ssm), proj,
                       jnp.repeat(p["d_skip"], HEAD_DIM).reshape(1, d_ssm), p["ssm_norm_w"],
                       tl=tr, col_z=2 * conv_ch, gw=d_ssm // SSM_GROUPS)

    x1 = _out_proj(out_a.reshape(m, conv_ch), out_b, p["w_out"], x2d, gate1, seq, tm=tm, tn=512)
    hn2, top_e, probs = _norm_mod(x1, p["norm2_w"], shift2, scale2, seq, tl=tr,
                                  router=(p["w_router"], p["b_router"]))
    return x1, hn2, top_e[:, :TOP_K], probs, new_a, new_b, h_fin


def kernel(x_prompt, x_sample, state_conv_a, state_conv_ssm, state_ssm, c_prompt, c_sample, norm1_w, norm2_w, final_norm_w, w_ada, b_ada, w_in, b_glu, w_dw, b_dw, ln_w, ln_b, w_xbc_conv, b_xbc_conv, dt_bias, a_log, d_skip, ssm_norm_w, w_out, w_router, b_router, w_gate, b_gate, w_up, b_up, w_down, b_down):
    layer = 0
    p = dict(norm1_w=norm1_w[layer], norm2_w=norm2_w[layer], w_in=w_in[layer], b_glu=b_glu[layer],
             w_dw=w_dw[layer], b_dw=b_dw[layer], ln_w=ln_w[layer], ln_b=ln_b[layer],
             w_xbc_conv=w_xbc_conv[layer], b_xbc_conv=b_xbc_conv[layer], dt_bias=dt_bias[layer],
             a_log=a_log[layer], d_skip=d_skip[layer], ssm_norm_w=ssm_norm_w[layer], w_out=w_out[layer],
             w_router=w_router[layer], b_router=b_router[layer])
    bp, sp, d = x_prompt.shape
    bs, ss, _ = x_sample.shape
    mp, ms = bp * sp, bs * ss
    n_experts = w_router.shape[-1]

    c_all = jnp.concatenate([c_prompt, c_sample], axis=0)
    n_c = c_all.shape[0]
    n_cp = (n_c + 7) // 8 * 8
    c_all = jnp.pad(c_all, ((0, n_cp - n_c), (0, 0)))
    mod = _mm(c_all, w_ada[layer], b_ada[layer].reshape(1, -1), tm=n_cp, tn=512, n_out=w_ada.shape[-1], silu_in=True)

    zeros = lambda a, b: jnp.zeros((b,) + a.shape[2:], a.dtype)
    x1p, hn2p, ep, pp, conv_a_p, conv_b_p, ssm_p = _group_front(
        x_prompt, mod[:bp], zeros(state_conv_a, bp), zeros(state_conv_ssm, bp), zeros(state_ssm, bp), p,
        tl=min(256, sp), tr=min(256, mp), tm=min(512, mp))
    x1s, hn2s, es, ps, conv_a_s, conv_b_s, ssm_s = _group_front(
        x_sample, mod[bp:bp + bs], state_conv_a[layer], state_conv_ssm[layer], state_ssm[layer], p,
        tl=ss, tr=min(256, ms), tm=min(512, ms))

    hn2 = jnp.concatenate([hn2p, hn2s], axis=0)
    top_e = jnp.concatenate([ep, es], axis=0)
    row_tok, pos, meta = _moe_dispatch(top_e, n_experts)
    xg = _dispatch_rows(hn2, row_tok, meta[1])
    rows_out = _moe_ffn_rows(xg, meta, w_gate[layer], b_gate[layer], w_up[layer], b_up[layer],
                             w_down[layer], b_down[layer], tf=min(1024, w_gate.shape[-1]), tn=min(1024, d))

    gate2_p = mod[:bp, 5 * d:]
    gate2_s = mod[bp:bp + bs, 5 * d:]
    y_p = _combine_final(x1p, rows_out, pos[:mp * TOP_K], pp, gate2_p, final_norm_w, sp, tl=min(128, mp))
    y_s = _combine_final(x1s, rows_out, pos[mp * TOP_K:], ps, gate2_s, final_norm_w, ss, tl=min(128, ms))
    y_p = y_p.reshape(bp, sp, d)
    y_s = y_s.reshape(bs, ss, d)
    return (y_p, y_s, conv_a_p[None], conv_b_p[None], ssm_p[None], conv_a_s[None], conv_b_s[None], ssm_s[None])
```
